```python
import jax, jax.numpy as jnp
from jax import lax
import numpy as np

D_MODEL = 1024
BATCH = 8
SEQ = 4096
DEPTH = 2

CHUNK = 64
Q_BLOCK = 128
D_MIX = D_MODEL
N_HEADS = 8
QK_NOPE = 64
QK_ROPE = 32
QK_DIM = QK_NOPE + QK_ROPE
V_DIM = 64
Q_LORA = 384
KV_LORA = 256
ATTN_W = N_HEADS * V_DIM
CONV_W = D_MIX - ATTN_W
CONV_K = 31
D_FF = 2816
ROPE_THETA = 10000.0
EPS = 1e-6
IN_COLS = Q_LORA + KV_LORA + QK_ROPE + 2 * CONV_W

kernel_name = "hymba_mla_conformer_conv_macaron"


def rms_norm(x, g):
    xf = x.astype(jnp.float32)
    y = xf * lax.rsqrt(jnp.mean(xf * xf, axis=-1, keepdims=True) + EPS)
    return (y * g.astype(jnp.float32)).astype(x.dtype)


def layer_norm(x, g, b):
    xf = x.astype(jnp.float32)
    mu = jnp.mean(xf, axis=-1, keepdims=True)
    xc = xf - mu
    y = xc * lax.rsqrt(jnp.mean(xc * xc, axis=-1, keepdims=True) + EPS)
    return (y * g.astype(jnp.float32) + b.astype(jnp.float32)).astype(x.dtype)


def swiglu(x, w_gate, w_up, w_down):
    return (jax.nn.silu(x @ w_gate) * (x @ w_up)) @ w_down


def rope_tables(seq_len):
    pos = jnp.arange(seq_len, dtype=jnp.float32)
    inv_freq = 1.0 / (ROPE_THETA ** (jnp.arange(0, QK_ROPE, 2, dtype=jnp.float32) / QK_ROPE))
    ang = pos[:, None] * inv_freq[None, :]
    return jnp.cos(ang), jnp.sin(ang)


def apply_rope(x, cos, sin):
    xf = x.astype(jnp.float32)
    half = QK_ROPE // 2
    x1, x2 = xf[..., :half], xf[..., half:]
    c, s = cos[None, :, None, :], sin[None, :, None, :]
    out = jnp.concatenate([x1 * c - x2 * s, x1 * s + x2 * c], axis=-1)
    return out.astype(x.dtype)


def chunk_causal_attention(q, k, v):
    b, h, s, _ = q.shape
    scale = QK_DIM ** -0.5
    outs = []
    for i in range(s // Q_BLOCK):
        q0 = i * Q_BLOCK
        k_end = q0 + Q_BLOCK
        qs = q[:, :, q0:k_end]
        ks = k[:, :, :k_end]
        vs = v[:, :, :k_end]
        scores = jnp.einsum('bhqd,bhkd->bhqk', qs, ks).astype(jnp.float32) * scale
        q_chunk = (q0 + jnp.arange(Q_BLOCK)) // CHUNK
        k_chunk = jnp.arange(k_end) // CHUNK
        allowed = k_chunk[None, :] <= q_chunk[:, None]
        scores = jnp.where(allowed[None, None], scores, -1e30)
        probs = jax.nn.softmax(scores, axis=-1).astype(vs.dtype)
        outs.append(jnp.einsum('bhqk,bhkd->bqhd', probs, vs))
    out = jnp.concatenate(outs, axis=1)
    return out.reshape(b, s, h * V_DIM)


def mla_group(c_q_raw, c_kv_raw, k_pe_raw, q_lat_g, w_uq, kv_lat_g, w_ukv, q_g, k_g, cos, sin):
    b, s, _ = c_q_raw.shape
    q = (rms_norm(c_q_raw, q_lat_g) @ w_uq).reshape(b, s, N_HEADS, QK_DIM)
    kv = (rms_norm(c_kv_raw, kv_lat_g) @ w_ukv).reshape(b, s, N_HEADS, QK_NOPE + V_DIM)
    k_nope, v = kv[..., :QK_NOPE], kv[..., QK_NOPE:]
    k_pe = jnp.broadcast_to(k_pe_raw[:, :, None, :], (b, s, N_HEADS, QK_ROPE))
    k = jnp.concatenate([k_nope, k_pe], axis=-1)
    q = rms_norm(q, q_g)
    k = rms_norm(k, k_g)
    q = jnp.concatenate([q[..., :QK_NOPE], apply_rope(q[..., QK_NOPE:], cos, sin)], axis=-1)
    k = jnp.concatenate([k[..., :QK_NOPE], apply_rope(k[..., QK_NOPE:], cos, sin)], axis=-1)
    q = jnp.transpose(q, (0, 2, 1, 3))
    k = jnp.transpose(k, (0, 2, 1, 3))
    v = jnp.transpose(v, (0, 2, 1, 3))
    return chunk_causal_attention(q, k, v)


def conv_group(u_raw, conv_w, conv_b, ln_g, ln_b):
    a, g = u_raw[..., :CONV_W], u_raw[..., CONV_W:]
    u = a * jax.nn.sigmoid(g)
    y = lax.conv_general_dilated(
        u, conv_w[:, None, :].astype(u.dtype),
        window_strides=(1,), padding=[(CONV_K - 1, 0)],
        dimension_numbers=('NWC', 'WIO', 'NWC'),
        feature_group_count=CONV_W)
    y = y + conv_b
    return jax.nn.silu(layer_norm(y, ln_g, ln_b))


def setup_inputs(seed: int = 0) -> dict:
    key = jax.random.key(seed)
    ks = jax.random.split(key, 24)
    L, D, F = DEPTH, D_MODEL, D_FF

    def w(k, shape, fan_in):
        return jax.random.normal(k, shape, jnp.float32) * (fan_in ** -0.5)

    def gain(k, shape):
        return 1.0 + 0.02 * jax.random.normal(k, shape, jnp.float32)

    def bias(k, shape):
        return 0.02 * jax.random.normal(k, shape, jnp.float32)

    return {
        "x": jax.random.normal(ks[0], (BATCH, SEQ, D), jnp.float32),
        "ffn1_norm": gain(ks[1], (L, D)),
        "ffn1_w_gate": w(ks[2], (L, D, F), D),
        "ffn1_w_up": w(ks[3], (L, D, F), D),
        "ffn1_w_down": w(ks[4], (L, F, D), F),
        "mix_norm": gain(ks[5], (L, D)),
        "w_in": w(ks[6], (L, D, IN_COLS), D),
        "q_latent_norm": gain(ks[7], (L, Q_LORA)),
        "w_uq": w(ks[8], (L, Q_LORA, N_HEADS * QK_DIM), Q_LORA),
        "kv_latent_norm": gain(ks[9], (L, KV_LORA)),
        "w_ukv": w(ks[10], (L, KV_LORA, N_HEADS * (QK_NOPE + V_DIM)), KV_LORA),
        "q_norm": gain(ks[11], (L, QK_DIM)),
        "k_norm": gain(ks[12], (L, QK_DIM)),
        "conv_w": w(ks[13], (L, CONV_K, CONV_W), CONV_K),
        "conv_b": bias(ks[14], (L, CONV_W)),
        "conv_ln_g": gain(ks[15], (L, CONV_W)),
        "conv_ln_b": bias(ks[16], (L, CONV_W)),
        "w_out": w(ks[17], (L, D_MIX, D), D_MIX),
        "ffn2_norm": gain(ks[18], (L, D)),
        "ffn2_w_gate": w(ks[19], (L, D, F), D),
        "ffn2_w_up": w(ks[20], (L, D, F), D),
        "ffn2_w_down": w(ks[21], (L, F, D), F),
        "post_norm": gain(ks[22], (L, D)),
    }


def reference(x, ffn1_norm, ffn1_w_gate, ffn1_w_up, ffn1_w_down, mix_norm, w_in,
              q_latent_norm, w_uq, kv_latent_norm, w_ukv, q_norm, k_norm,
              conv_w, conv_b, conv_ln_g, conv_ln_b, w_out,
              ffn2_norm, ffn2_w_gate, ffn2_w_up, ffn2_w_down, post_norm):
    cos, sin = rope_tables(x.shape[1])
    o_kv = Q_LORA
    o_pe = Q_LORA + KV_LORA
    o_cv = Q_LORA + KV_LORA + QK_ROPE
    for l in range(DEPTH):
        x = x + 0.5 * swiglu(rms_norm(x, ffn1_norm[l]), ffn1_w_gate[l], ffn1_w_up[l], ffn1_w_down[l])
        h = rms_norm(x, mix_norm[l])
        p = h @ w_in[l]
        attn = mla_group(p[..., :o_kv], p[..., o_kv:o_pe], p[..., o_pe:o_cv],
                         q_latent_norm[l], w_uq[l], kv_latent_norm[l], w_ukv[l],
                         q_norm[l], k_norm[l], cos, sin)
        conv = conv_group(p[..., o_cv:], conv_w[l], conv_b[l], conv_ln_g[l], conv_ln_b[l])
        x = x + jnp.concatenate([attn, conv], axis=-1) @ w_out[l]
        x = x + 0.5 * swiglu(rms_norm(x, ffn2_norm[l]), ffn2_w_gate[l], ffn2_w_up[l], ffn2_w_down[l])
        x = rms_norm(x, post_norm[l])
    return x
```

```python
import functools

import jax
import jax.numpy as jnp
from jax import lax
from jax.experimental import pallas as pl
from jax.experimental.pallas import tpu as pltpu

D_MODEL = 1024
SEQ = 4096
CHUNK = 64
N_HEADS = 8
QK_NOPE = 64
QK_ROPE = 32
QK_DIM = QK_NOPE + QK_ROPE
V_DIM = 64
Q_LORA = 384
KV_LORA = 256
ATTN_W = N_HEADS * V_DIM
CONV_W = D_MODEL - ATTN_W
CONV_K = 31
D_FF = 2816
ROPE_THETA = 10000.0
EPS = 1e-6
NEG = -1e30

LANES = 128
HEAD_SLAB = LANES
HALO = 32
VMEM_LIMIT = 56 * 1024 * 1024

TM_FFN = 512
TM_PROJ = 256
TM_MIX = 256
TQ = 256
TK = 256
CONV_ROWS = 32
F_CHUNKS = ((0, 1536), (1536, D_FF))

BF16 = jnp.bfloat16
F32 = jnp.float32


def _rms(x, g):
    y = x * lax.rsqrt(jnp.mean(x * x, axis=-1, keepdims=True) + EPS)
    return y * g


def _dot(a, b):
    return jnp.dot(a, b, preferred_element_type=F32)


def _const_spec(shape):
    return pl.BlockSpec(shape, lambda *_: (0,) * len(shape), pipeline_mode=pl.Buffered(1))


def _params(n_axes):
    return pltpu.CompilerParams(dimension_semantics=("arbitrary",) * n_axes, vmem_limit_bytes=VMEM_LIMIT)


def _ffn_kernel(x_ref, g_ref, wg_ref, wu_ref, wd_ref, *rest, post_norm):
    if post_norm:
        pg_ref, o_ref = rest
    else:
        (o_ref,) = rest
    x = x_ref[...]
    h = _rms(x, g_ref[...]).astype(BF16)
    y = jnp.zeros_like(x)
    for lo, hi in F_CHUNKS:
        gate = _dot(h, wg_ref[:, lo:hi])
        up = _dot(h, wu_ref[:, lo:hi])
        a = (gate * jax.nn.sigmoid(gate) * up).astype(BF16)
        y = y + _dot(a, wd_ref[lo:hi, :])
    out = x + 0.5 * y
    if post_norm:
        out = _rms(out, pg_ref[...])
    o_ref[...] = out


def _ffn(x2d, g, wg, wu, wd, post_g=None):
    n, d = x2d.shape
    row = pl.BlockSpec((TM_FFN, d), lambda i: (i, 0))
    in_specs = [row, _const_spec((1, d)), _const_spec(wg.shape), _const_spec(wu.shape), _const_spec(wd.shape)]
    args = [x2d, g.reshape(1, d), wg, wu, wd]
    if post_g is not None:
        in_specs.append(_const_spec((1, d)))
        args.append(post_g.reshape(1, d))
    return pl.pallas_call(
        functools.partial(_ffn_kernel, post_norm=post_g is not None),
        grid=(n // TM_FFN,),
        in_specs=in_specs,
        out_specs=row,
        out_shape=jax.ShapeDtypeStruct((n, d), F32),
        compiler_params=_params(1),
        name="ffn_post" if post_g is not None else "ffn",
    )(*args)


def _rope(x, c, sa, sb):
    return x * c + pltpu.roll(x, HEAD_SLAB - QK_ROPE // 2, 1) * sa + pltpu.roll(x, QK_ROPE // 2, 1) * sb


def _proj_kernel(x_ref, g_ref, wq_ref, wkv_ref, wpe_ref, wcv_ref, qlg_ref, wuq_ref, kvlg_ref, wk_ref, wv_ref,
                 qg_ref, kg_ref, c_ref, sa_ref, sb_ref, q_ref, k_ref, v_ref, u_ref):
    h = _rms(x_ref[...], g_ref[...]).astype(BF16)
    c, sa, sb = c_ref[...], sa_ref[...], sb_ref[...]

    cq = _rms(_dot(h, wq_ref[...]), qlg_ref[...]).astype(BF16)
    qf = _dot(cq, wuq_ref[...])
    qg = qg_ref[...]
    for hd in range(N_HEADS):
        sl = slice(hd * HEAD_SLAB, (hd + 1) * HEAD_SLAB)
        qh = qf[:, sl]
        qn = qh * lax.rsqrt(jnp.sum(qh * qh, axis=-1, keepdims=True) / QK_DIM + EPS) * qg
        q_ref[:, sl] = _rope(qn, c, sa, sb).astype(BF16)

    ckv = _rms(_dot(h, wkv_ref[...]), kvlg_ref[...]).astype(BF16)
    kf = _dot(ckv, wk_ref[...])
    v_ref[...] = _dot(ckv, wv_ref[...]).astype(BF16)
    kpe = _dot(h, wpe_ref[...])
    kg = kg_ref[...]
    pe_ss = jnp.sum(kpe * kpe, axis=-1, keepdims=True)
    pe_rot = _rope(kpe * kg, c, sa, sb)
    for hd in range(N_HEADS):
        sl = slice(hd * HEAD_SLAB, (hd + 1) * HEAD_SLAB)
        kh = kf[:, sl]
        rs = lax.rsqrt((jnp.sum(kh * kh, axis=-1, keepdims=True) + pe_ss) / QK_DIM + EPS)
        k_ref[:, sl] = ((kh * kg + pe_rot) * rs).astype(BF16)

    cv = _dot(h, wcv_ref[...])
    u_ref[...] = cv[:, :CONV_W] * jax.nn.sigmoid(cv[:, CONV_W:])


def _proj(x2d, g, wq, wkv, wpe, wcv, qlg, wuq, kvlg, wk, wv, qg, kg, c_tab, sa_tab, sb_tab):
    n, d = x2d.shape
    tm = TM_PROJ
    s_blocks = SEQ // tm
    row = lambda w: pl.BlockSpec((tm, w), lambda i: (i, 0))
    tab = pl.BlockSpec((tm, HEAD_SLAB), lambda i: (i % s_blocks, 0))
    consts = [g, wq, wkv, wpe, wcv, qlg, wuq, kvlg, wk, wv, qg, kg]
    in_specs = [row(d)] + [_const_spec(a.shape) for a in consts] + [tab, tab, tab]
    hs = N_HEADS * HEAD_SLAB
    return pl.pallas_call(
        _proj_kernel,
        grid=(n // tm,),
        in_specs=in_specs,
        out_specs=[row(hs), row(hs), row(ATTN_W), row(CONV_W)],
        out_shape=[jax.ShapeDtypeStruct((n, hs), BF16), jax.ShapeDtypeStruct((n, hs), BF16),
                   jax.ShapeDtypeStruct((n, ATTN_W), BF16), jax.ShapeDtypeStruct((n, CONV_W), F32)],
        compiler_params=_params(1),
        name="proj",
    )(x2d, *consts, c_tab, sa_tab, sb_tab)


def _attn_kernel(q_ref, k_ref, v_ref, o_ref):
    qi = pl.program_id(2)
    scale = QK_DIM ** -0.5
    row_chunk = lax.broadcasted_iota(jnp.int32, (TQ, TK), 0) // CHUNK
    col_chunk = lax.broadcasted_iota(jnp.int32, (TQ, TK), 1) // CHUNK
    allowed = col_chunk <= row_chunk

    outs = []
    for hh in range(2):
        sl = slice(hh * HEAD_SLAB, (hh + 1) * HEAD_SLAB)
        q = q_ref[:, sl]

        def step(kj, carry, masked, sl=sl, q=q):
            m, l, acc = carry
            k0 = pl.multiple_of(kj * TK, TK)
            k = k_ref[pl.ds(k0, TK), sl]
            v = v_ref[pl.ds(k0, TK), :]
            s = lax.dot_general(q, k, (((1,), (1,)), ((), ())), preferred_element_type=F32) * scale
            if masked:
                s = jnp.where(allowed, s, NEG)
            m_new = jnp.maximum(m, jnp.max(s, axis=-1, keepdims=True))
            alpha = jnp.exp(m - m_new)
            p = jnp.exp(s - m_new)
            l = alpha * l + jnp.sum(p, axis=-1, keepdims=True)
            acc = alpha * acc + _dot(p.astype(BF16), v)
            return m_new, l, acc

        init = (jnp.full((TQ, 1), NEG, F32), jnp.zeros((TQ, 1), F32), jnp.zeros((TQ, 2 * V_DIM), F32))
        carry = lax.fori_loop(0, qi, functools.partial(step, masked=False), init)
        _, l, acc = step(qi, carry, masked=True)
        outs.append(acc / l)

    lane = lax.broadcasted_iota(jnp.int32, (TQ, 2 * V_DIM), 1)
    o_ref[...] = jnp.where(lane < V_DIM, outs[0], outs[1]).astype(BF16)


def _attn(q, k, v, batch):
    hs = N_HEADS * HEAD_SLAB
    q3, k3, v3 = q.reshape(batch, SEQ, hs), k.reshape(batch, SEQ, hs), v.reshape(batch, SEQ, ATTN_W)
    out = pl.pallas_call(
        _attn_kernel,
        grid=(batch, N_HEADS // 2, SEQ // TQ),
        in_specs=[pl.BlockSpec((None, TQ, 2 * HEAD_SLAB), lambda b, hp, qi: (b, qi, hp)),
                  pl.BlockSpec((None, SEQ, 2 * HEAD_SLAB), lambda b, hp, qi: (b, 0, hp)),
                  pl.BlockSpec((None, SEQ, 2 * V_DIM), lambda b, hp, qi: (b, 0, hp))],
        out_specs=pl.BlockSpec((None, TQ, 2 * V_DIM), lambda b, hp, qi: (b, qi, hp)),
        out_shape=jax.ShapeDtypeStruct((batch, SEQ, ATTN_W), BF16),
        compiler_params=_params(3),
        name="attn",
    )(q3, k3, v3)
    return out.reshape(batch * SEQ, ATTN_W)


def _mix_out_kernel(uprev_ref, u_ref, attn_ref, x_ref, cw_ref, cb_ref, lg_ref, lb_ref, woa_ref, woc_ref,
                    o_ref, ext_ref, conv_ref):
    tm = TM_MIX
    first = (pl.program_id(0) % (SEQ // tm)) == 0
    ext_ref[0:HALO, :] = jnp.where(first, 0.0, uprev_ref[...])
    ext_ref[HALO:HALO + tm, :] = u_ref[...]

    cb, lg, lb = cb_ref[...], lg_ref[...], lb_ref[...]
    base = HALO - (CONV_K - 1)
    for r in range(0, tm, CONV_ROWS):
        acc = jnp.zeros((CONV_ROWS, CONV_W), F32)
        for j in range(CONV_K):
            acc = acc + cw_ref[j:j + 1, :] * ext_ref[r + base + j:r + base + j + CONV_ROWS, :]
        y = acc + cb
        mu = jnp.mean(y, axis=-1, keepdims=True)
        yc = y - mu
        z = yc * lax.rsqrt(jnp.mean(yc * yc, axis=-1, keepdims=True) + EPS) * lg + lb
        conv_ref[r:r + CONV_ROWS, :] = (z * jax.nn.sigmoid(z)).astype(BF16)

    y = _dot(attn_ref[...], woa_ref[...]) + _dot(conv_ref[...], woc_ref[...])
    o_ref[...] = x_ref[...] + y


def _mix_out(u, attn, x2d, cw, cb, lg, lb, woa, woc):
    n, d = x2d.shape
    tm = TM_MIX
    row = lambda w: pl.BlockSpec((tm, w), lambda i: (i, 0))
    halo = pl.BlockSpec((HALO, CONV_W), lambda i: (jnp.maximum(i * (tm // HALO) - 1, 0), 0))
    consts = [cw, cb, lg, lb, woa, woc]
    return pl.pallas_call(
        _mix_out_kernel,
        grid=(n // tm,),
        in_specs=[halo, row(CONV_W), row(ATTN_W), row(d)] + [_const_spec(a.shape) for a in consts],
        out_specs=row(d),
        out_shape=jax.ShapeDtypeStruct((n, d), F32),
        scratch_shapes=[pltpu.VMEM((HALO + tm, CONV_W), F32), pltpu.VMEM((tm, CONV_W), BF16)],
        compiler_params=_params(1),
        name="mix_out",
    )(u, u, attn, x2d, *consts)


def _rope_slab_tables(seq_len):
    pos = jnp.arange(seq_len, dtype=F32)
    inv_freq = 1.0 / (ROPE_THETA ** (jnp.arange(0, QK_ROPE, 2, dtype=F32) / QK_ROPE))
    ang = pos[:, None] * inv_freq[None, :]
    cos, sin = jnp.cos(ang), jnp.sin(ang)
    half = QK_ROPE // 2
    ones = jnp.ones((seq_len, QK_NOPE), F32)
    z = lambda w: jnp.zeros((seq_len, w), F32)
    tail = HEAD_SLAB - QK_DIM
    c_tab = jnp.concatenate([ones, cos, cos, z(tail)], axis=1)
    sa_tab = jnp.concatenate([z(QK_NOPE), -sin, z(half), z(tail)], axis=1)
    sb_tab = jnp.concatenate([z(QK_NOPE), z(half), sin, z(tail)], axis=1)
    return c_tab, sa_tab, sb_tab


def _head_slabs(w, width, offset=0):
    kdim = w.shape[0]
    w3 = w.reshape(kdim, N_HEADS, width)
    w3 = jnp.pad(w3, ((0, 0), (0, 0), (offset, HEAD_SLAB - width - offset)))
    return w3.reshape(kdim, N_HEADS * HEAD_SLAB)


def _slab_vec(g, offset=0):
    return jnp.pad(g, (offset, HEAD_SLAB - g.shape[0] - offset)).reshape(1, HEAD_SLAB)


def kernel(x, ffn1_norm, ffn1_w_gate, ffn1_w_up, ffn1_w_down, mix_norm, w_in, q_latent_norm, w_uq,
           kv_latent_norm, w_ukv, q_norm, k_norm, conv_w, conv_b, conv_ln_g, conv_ln_b, w_out,
           ffn2_norm, ffn2_w_gate, ffn2_w_up, ffn2_w_down, post_norm):
    batch, seq, d = x.shape
    assert (seq, d) == (SEQ, D_MODEL)
    depth = ffn1_norm.shape[0]
    tabs = _rope_slab_tables(seq)
    o_kv, o_pe, o_cv = Q_LORA, Q_LORA + KV_LORA, Q_LORA + KV_LORA + QK_ROPE
    bf = lambda a: a.astype(BF16)
    row = lambda a: a.reshape(1, -1)

    x2d = x.reshape(batch * seq, d)
    for l in range(depth):
        x2d = _ffn(x2d, ffn1_norm[l], bf(ffn1_w_gate[l]), bf(ffn1_w_up[l]), bf(ffn1_w_down[l]))

        wi = w_in[l]
        wpe = jnp.pad(wi[:, o_pe:o_cv], ((0, 0), (QK_NOPE, HEAD_SLAB - QK_DIM)))
        ukv = w_ukv[l].reshape(KV_LORA, N_HEADS, QK_NOPE + V_DIM)
        wk = _head_slabs(ukv[:, :, :QK_NOPE].reshape(KV_LORA, N_HEADS * QK_NOPE), QK_NOPE)
        wv = ukv[:, :, QK_NOPE:].reshape(KV_LORA, ATTN_W)
        q, k, v, u = _proj(
            x2d, row(mix_norm[l]), bf(wi[:, :o_kv]), bf(wi[:, o_kv:o_pe]), bf(wpe), bf(wi[:, o_cv:]),
            row(q_latent_norm[l]), bf(_head_slabs(w_uq[l], QK_DIM)), row(kv_latent_norm[l]), bf(wk), bf(wv),
            _slab_vec(q_norm[l]), _slab_vec(k_norm[l]), *tabs)

        attn = _attn(q, k, v, batch)

        wo = w_out[l]
        x2d = _mix_out(u, attn, x2d, conv_w[l], row(conv_b[l]), row(conv_ln_g[l]), row(conv_ln_b[l]),
                       bf(wo[:ATTN_W]), bf(wo[ATTN_W:]))

        x2d = _ffn(x2d, ffn2_norm[l], bf(ffn2_w_gate[l]), bf(ffn2_w_up[l]), bf(ffn2_w_down[l]),
                   post_g=post_norm[l])
    return x2d.reshape(batch, seq, d)
```

```python
import functools

import jax
import jax.numpy as jnp
from jax import lax
from jax.experimental import pallas as pl
from jax.experimental.pallas import tpu as pltpu

D_MODEL = 1024
SEQ = 4096
CHUNK = 64
N_HEADS = 8
QK_NOPE = 64
QK_ROPE = 32
QK_DIM = QK_NOPE + QK_ROPE
V_DIM = 64
Q_LORA = 384
KV_LORA = 256
ATTN_W = N_HEADS * V_DIM
CONV_W = D_MODEL - ATTN_W
CONV_K = 31
D_FF = 2816
ROPE_THETA = 10000.0
EPS = 1e-6
NEG = -1e30

LANES = 128
HEAD_SLAB = LANES
HALO = 32
VMEM_LIMIT = 56 * 1024 * 1024

TM_FFN = 512
TM_PROJ = 256
TM_MIX = 256
TQ = 256
TK = 256
CONV_ROWS = 32
SUBLANES = 8
V_ROWS = 80
LOG2E = 1.4426950408889634
F_CHUNKS = ((0, 1536), (1536, D_FF))

BF16 = jnp.bfloat16
F32 = jnp.float32


def _rms(x, g):
    y = x * lax.rsqrt(jnp.mean(x * x, axis=-1, keepdims=True) + EPS)
    return y * g


def _dot(a, b):
    return jnp.dot(a, b, preferred_element_type=F32)


def _const_spec(shape):
    return pl.BlockSpec(shape, lambda *_: (0,) * len(shape), pipeline_mode=pl.Buffered(1))


def _params(n_axes):
    return pltpu.CompilerParams(dimension_semantics=("arbitrary",) * n_axes, vmem_limit_bytes=VMEM_LIMIT)


def _ffn_kernel(x_ref, g_ref, wg_ref, wu_ref, wd_ref, *rest, post_norm):
    if post_norm:
        pg_ref, o_ref = rest
    else:
        (o_ref,) = rest
    x = x_ref[...]
    h = _rms(x, g_ref[...]).astype(BF16)
    y = jnp.zeros_like(x)
    for lo, hi in F_CHUNKS:
        gate = _dot(h, wg_ref[:, lo:hi])
        up = _dot(h, wu_ref[:, lo:hi])
        a = (gate * jax.nn.sigmoid(gate) * up).astype(BF16)
        y = y + _dot(a, wd_ref[lo:hi, :])
    out = x + 0.5 * y
    if post_norm:
        out = _rms(out, pg_ref[...])
    o_ref[...] = out


def _ffn(x2d, g, wg, wu, wd, post_g=None):
    n, d = x2d.shape
    row = pl.BlockSpec((TM_FFN, d), lambda i: (i, 0))
    in_specs = [row, _const_spec((1, d)), _const_spec(wg.shape), _const_spec(wu.shape), _const_spec(wd.shape)]
    args = [x2d, g.reshape(1, d), wg, wu, wd]
    if post_g is not None:
        in_specs.append(_const_spec((1, d)))
        args.append(post_g.reshape(1, d))
    return pl.pallas_call(
        functools.partial(_ffn_kernel, post_norm=post_g is not None),
        grid=(n // TM_FFN,),
        in_specs=in_specs,
        out_specs=row,
        out_shape=jax.ShapeDtypeStruct((n, d), F32),
        compiler_params=_params(1),
        name="ffn_post" if post_g is not None else "ffn",
    )(*args)


def _rope(x, c, sa, sb):
    return x * c + pltpu.roll(x, HEAD_SLAB - QK_ROPE // 2, 1) * sa + pltpu.roll(x, QK_ROPE // 2, 1) * sb


def _proj_kernel(x_ref, g_ref, wq_ref, wkv_ref, wpe_ref, wcv_ref, qlg_ref, wuq_ref, kvlg_ref, wk_ref, wvt_ref,
                 qg_ref, kg_ref, c_ref, sa_ref, sb_ref, q_ref, k_ref, vt_ref, u_ref):
    h = _rms(x_ref[...], g_ref[...]).astype(BF16)
    c, sa, sb = c_ref[...], sa_ref[...], sb_ref[...]

    cq = _rms(_dot(h, wq_ref[...]), qlg_ref[...]).astype(BF16)
    qf = _dot(cq, wuq_ref[...])
    qg = qg_ref[...]
    for hd in range(N_HEADS):
        sl = slice(hd * HEAD_SLAB, (hd + 1) * HEAD_SLAB)
        qh = qf[:, sl]
        qn = qh * lax.rsqrt(jnp.sum(qh * qh, axis=-1, keepdims=True) / QK_DIM + EPS) * qg
        q_ref[:, sl] = _rope(qn, c, sa, sb).astype(BF16)

    ckv = _rms(_dot(h, wkv_ref[...]), kvlg_ref[...]).astype(BF16)
    kf = _dot(ckv, wk_ref[...])
    vt = lax.dot_general(wvt_ref[...], ckv, (((1,), (1,)), ((), ())), preferred_element_type=F32)
    pad_rows = V_ROWS - V_DIM
    ones_row = jnp.where(lax.broadcasted_iota(jnp.int32, (pad_rows, vt.shape[1]), 0) == 0, 1.0, 0.0).astype(BF16)
    for hd in range(N_HEADS):
        vt_ref[hd, 0:V_DIM, :] = vt[hd * V_DIM:(hd + 1) * V_DIM, :].astype(BF16)
        vt_ref[hd, V_DIM:V_ROWS, :] = ones_row
    kpe =_dot(h, wpe_ref[...])
    kg = kg_ref[...]
    pe_ss = jnp.sum(kpe * kpe, axis=-1, keepdims=True)
    pe_rot = _rope(kpe * kg, c, sa, sb)
    for hd in range(N_HEADS):
        sl = slice(hd * HEAD_SLAB, (hd + 1) * HEAD_SLAB)
        kh = kf[:, sl]
        rs = lax.rsqrt((jnp.sum(kh * kh, axis=-1, keepdims=True) + pe_ss) / QK_DIM + EPS)
        k_ref[:, sl] = ((kh * kg + pe_rot) * rs).astype(BF16)

    cv = _dot(h, wcv_ref[...])
    u_ref[...] = cv[:, :CONV_W] * jax.nn.sigmoid(cv[:, CONV_W:])


def _proj(x2d, g, wq, wkv, wpe, wcv, qlg, wuq, kvlg, wk, wvt, qg, kg, c_tab, sa_tab, sb_tab):
    n, d = x2d.shape
    tm = TM_PROJ
    assert tm == TK
    s_blocks = SEQ // tm
    batch = n // SEQ
    row = lambda w: pl.BlockSpec((tm, w), lambda i: (i, 0))
    tab = pl.BlockSpec((tm, HEAD_SLAB), lambda i: (i % s_blocks, 0))
    vt_spec = pl.BlockSpec((None, N_HEADS, None, V_ROWS, TK), lambda i: (i // s_blocks, 0, i % s_blocks, 0, 0))
    consts = [g, wq, wkv, wpe, wcv, qlg, wuq, kvlg, wk, wvt, qg, kg]
    in_specs = [row(d)] + [_const_spec(a.shape) for a in consts] + [tab, tab, tab]
    hs = N_HEADS * HEAD_SLAB
    return pl.pallas_call(
        _proj_kernel,
        grid=(n // tm,),
        in_specs=in_specs,
        out_specs=[row(hs), row(hs), vt_spec, row(CONV_W)],
        out_shape=[jax.ShapeDtypeStruct((n, hs), BF16), jax.ShapeDtypeStruct((n, hs), BF16),
                   jax.ShapeDtypeStruct((batch, N_HEADS, s_blocks, V_ROWS, TK), BF16),
                   jax.ShapeDtypeStruct((n, CONV_W), F32)],
        compiler_params=_params(1),
        name="proj",
    )(x2d, *consts, c_tab, sa_tab, sb_tab)


def _attn_kernel(q_ref, k_ref, vt_ref, o_ref, sta_ref, stb_ref, m_ref, acc_ref):
    qi = pl.program_id(1)
    m_ref[...] = jnp.full(m_ref.shape, NEG, F32)
    acc_ref[...] = jnp.zeros(acc_ref.shape, F32)
    key_chunk = lax.broadcasted_iota(jnp.int32, (TK, TQ), 0) // CHUNK
    qry_chunk = lax.broadcasted_iota(jnp.int32, (TK, TQ), 1) // CHUNK
    allowed = key_chunk <= qry_chunk

    def scores(kj, hd, dst_ref):
        k0 = pl.multiple_of(kj * TK, TK)
        sl = slice(hd * HEAD_SLAB, (hd + 1) * HEAD_SLAB)
        dst_ref[hd] = lax.dot_general(k_ref[pl.ds(k0, TK), sl], q_ref[:, sl], (((1,), (1,)), ((), ())),
                                      preferred_element_type=F32)

    def softmax_pv(kj, hd, src_ref, masked):
        st = src_ref[hd]
        if masked:
            st = jnp.where(allowed, st, NEG)
        m_old = m_ref[hd]
        m_new = jnp.maximum(m_old, jnp.max(st, axis=0, keepdims=True))
        p = jnp.exp2(st - m_new).astype(BF16)
        acc_ref[hd] = jnp.exp2(m_old - m_new) * acc_ref[hd] + _dot(vt_ref[hd, kj], p)
        m_ref[hd] = m_new

    def stage(kj, cur_ref, nxt_ref, masked=False):
        for hd in range(N_HEADS):
            if nxt_ref is not None:
                scores(kj + 1, hd, nxt_ref)
            softmax_pv(kj, hd, cur_ref, masked)

    for hd in range(N_HEADS):
        scores(0, hd, sta_ref)

    def body(t, carry):
        stage(2 * t, sta_ref, stb_ref)
        stage(2 * t + 1, stb_ref, sta_ref)
        return carry

    lax.fori_loop(0, qi // 2, body, 0)

    @pl.when(qi % 2 == 1)
    def _():
        stage(qi - 1, sta_ref, stb_ref)
        stage(qi, stb_ref, None, masked=True)

    @pl.when(qi % 2 == 0)
    def _():
        stage(qi, sta_ref, None, masked=True)

    for pair in range(N_HEADS // 2):
        halves = []
        for hd in (2 * pair, 2 * pair + 1):
            acc = acc_ref[hd]
            halves.append(acc[:V_DIM] / acc[V_DIM:V_DIM + 1])
        pair_t = jnp.concatenate(halves, axis=0)
        o_ref[:, pair * 2 * V_DIM:(pair + 1) * 2 * V_DIM] = pair_t.T.astype(BF16)


def _attn(q, k, vt, batch):
    hs = N_HEADS * HEAD_SLAB
    q3, k3 = q.reshape(batch, SEQ, hs), k.reshape(batch, SEQ, hs)
    out = pl.pallas_call(
        _attn_kernel,
        grid=(batch, SEQ // TQ),
        in_specs=[pl.BlockSpec((None, TQ, hs), lambda b, qi: (b, qi, 0)),
                  pl.BlockSpec((None, SEQ, hs), lambda b, qi: (b, 0, 0)),
                  pl.BlockSpec((None, N_HEADS, SEQ // TK, V_ROWS, TK), lambda b, qi: (b, 0, 0, 0, 0))],
        out_specs=pl.BlockSpec((None, TQ, ATTN_W), lambda b, qi: (b, qi, 0)),
        out_shape=jax.ShapeDtypeStruct((batch, SEQ, ATTN_W), BF16),
        scratch_shapes=[pltpu.VMEM((N_HEADS, TK, TQ), F32), pltpu.VMEM((N_HEADS, TK, TQ), F32),
                        pltpu.VMEM((N_HEADS, 1, TQ), F32),
                        pltpu.VMEM((N_HEADS, V_ROWS, TQ), F32)],
        compiler_params=_params(2),
        name="attn",
    )(q3, k3, vt)
    return out.reshape(batch * SEQ, ATTN_W)


def _mix_out_kernel(uprev_ref, u_ref, attn_ref, x_ref, cw_ref, cb_ref, lg_ref, lb_ref, woa_ref, woc_ref,
                    o_ref, ext_ref, conv_ref):
    tm = TM_MIX
    first = (pl.program_id(0) % (SEQ // tm)) == 0
    ext_ref[0, 0:HALO, :] = jnp.where(first, 0.0, uprev_ref[...])
    ext_ref[0, HALO:HALO + tm, :] = u_ref[...]
    span = HALO + tm - SUBLANES
    for b in range(1, SUBLANES):
        ext_ref[b, 0:span, :] = ext_ref[0, b:b + span, :]

    cb, lg, lb = cb_ref[...], lg_ref[...], lb_ref[...]
    base = HALO - (CONV_K - 1)
    groups = CONV_ROWS // SUBLANES
    for r in range(0, tm, CONV_ROWS):
        accs = [jnp.zeros((SUBLANES, CONV_W), F32) for _ in range(groups)]
        for j in range(CONV_K):
            b = (base + j) % SUBLANES
            a = r + base + j - b
            w = cw_ref[j]
            for g in range(groups):
                accs[g] = accs[g] + w * ext_ref[b, a + g * SUBLANES:a + (g + 1) * SUBLANES, :]
        y = jnp.concatenate(accs, axis=0) + cb
        mu = jnp.mean(y, axis=-1, keepdims=True)
        yc = y - mu
        z = yc * lax.rsqrt(jnp.mean(yc * yc, axis=-1, keepdims=True) + EPS) * lg + lb
        conv_ref[r:r + CONV_ROWS, :] = (z * jax.nn.sigmoid(z)).astype(BF16)

    y = _dot(attn_ref[...], woa_ref[...]) + _dot(conv_ref[...], woc_ref[...])
    o_ref[...] = x_ref[...] + y


def _mix_out(u, attn, x2d, cw, cb, lg, lb, woa, woc):
    n, d = x2d.shape
    tm = TM_MIX
    row = lambda w: pl.BlockSpec((tm, w), lambda i: (i, 0))
    halo = pl.BlockSpec((HALO, CONV_W), lambda i: (jnp.maximum(i * (tm // HALO) - 1, 0), 0))
    consts = [cw, cb, lg, lb, woa, woc]
    return pl.pallas_call(
        _mix_out_kernel,
        grid=(n // tm,),
        in_specs=[halo, row(CONV_W), row(ATTN_W), row(d)] + [_const_spec(a.shape) for a in consts],
        out_specs=row(d),
        out_shape=jax.ShapeDtypeStruct((n, d), F32),
        scratch_shapes=[pltpu.VMEM((SUBLANES, HALO + tm, CONV_W), F32), pltpu.VMEM((tm, CONV_W), BF16)],
        compiler_params=_params(1),
        name="mix_out",
    )(u, u, attn, x2d, *consts)


def _rope_slab_tables(seq_len):
    pos = jnp.arange(seq_len, dtype=F32)
    inv_freq = 1.0 / (ROPE_THETA ** (jnp.arange(0, QK_ROPE, 2, dtype=F32) / QK_ROPE))
    ang = pos[:, None] * inv_freq[None, :]
    cos, sin = jnp.cos(ang), jnp.sin(ang)
    half = QK_ROPE // 2
    ones = jnp.ones((seq_len, QK_NOPE), F32)
    z = lambda w: jnp.zeros((seq_len, w), F32)
    tail = HEAD_SLAB - QK_DIM
    c_tab = jnp.concatenate([ones, cos, cos, z(tail)], axis=1)
    sa_tab = jnp.concatenate([z(QK_NOPE), -sin, z(half), z(tail)], axis=1)
    sb_tab = jnp.concatenate([z(QK_NOPE), z(half), sin, z(tail)], axis=1)
    return c_tab, sa_tab, sb_tab


def _head_slabs(w, width, offset=0):
    kdim = w.shape[0]
    w3 = w.reshape(kdim, N_HEADS, width)
    w3 = jnp.pad(w3, ((0, 0), (0, 0), (offset, HEAD_SLAB - width - offset)))
    return w3.reshape(kdim, N_HEADS * HEAD_SLAB)


def _slab_vec(g, offset=0):
    return jnp.pad(g, (offset, HEAD_SLAB - g.shape[0] - offset)).reshape(1, HEAD_SLAB)


def kernel(x, ffn1_norm, ffn1_w_gate, ffn1_w_up, ffn1_w_down, mix_norm, w_in, q_latent_norm, w_uq,
           kv_latent_norm, w_ukv, q_norm, k_norm, conv_w, conv_b, conv_ln_g, conv_ln_b, w_out,
           ffn2_norm, ffn2_w_gate, ffn2_w_up, ffn2_w_down, post_norm):
    batch, seq, d = x.shape
    assert (seq, d) == (SEQ, D_MODEL)
    depth = ffn1_norm.shape[0]
    tabs = _rope_slab_tables(seq)
    o_kv, o_pe, o_cv = Q_LORA, Q_LORA + KV_LORA, Q_LORA + KV_LORA + QK_ROPE
    bf = lambda a: a.astype(BF16)
    row = lambda a: a.reshape(1, -1)

    x2d = x.reshape(batch * seq, d)
    for l in range(depth):
        x2d = _ffn(x2d, ffn1_norm[l], bf(ffn1_w_gate[l]), bf(ffn1_w_up[l]), bf(ffn1_w_down[l]))

        wi = w_in[l]
        wpe = jnp.pad(wi[:, o_pe:o_cv], ((0, 0), (QK_NOPE, HEAD_SLAB - QK_DIM)))
        ukv = w_ukv[l].reshape(KV_LORA, N_HEADS, QK_NOPE + V_DIM)
        wk = _head_slabs(ukv[:, :, :QK_NOPE].reshape(KV_LORA, N_HEADS * QK_NOPE), QK_NOPE)
        wvt = ukv[:, :, QK_NOPE:].reshape(KV_LORA, ATTN_W).T
        qg = _slab_vec(q_norm[l] * (QK_DIM ** -0.5 * LOG2E))
        q, k, vt, u = _proj(
            x2d, row(mix_norm[l]), bf(wi[:, :o_kv]), bf(wi[:, o_kv:o_pe]), bf(wpe), bf(wi[:, o_cv:]),
            row(q_latent_norm[l]), bf(_head_slabs(w_uq[l], QK_DIM)), row(kv_latent_norm[l]), bf(wk), bf(wvt),
            qg, _slab_vec(k_norm[l]), *tabs)

        attn = _attn(q, k, vt, batch)

        wo = w_out[l]
        cw = jnp.broadcast_to(conv_w[l][:, None, :], (CONV_K, SUBLANES, CONV_W))
        x2d = _mix_out(u, attn, x2d, cw, row(conv_b[l]), row(conv_ln_g[l]), row(conv_ln_b[l]),
                       bf(wo[:ATTN_W]), bf(wo[ATTN_W:]))

        x2d = _ffn(x2d, ffn2_norm[l], bf(ffn2_w_gate[l]), bf(ffn2_w_up[l]), bf(ffn2_w_down[l]),
                   post_g=post_norm[l])
    return x2d.reshape(batch, seq, d)
```

```python
import functools

import jax
import jax.numpy as jnp
from jax import lax
from jax.experimental import pallas as pl
from jax.experimental.pallas import tpu as pltpu

D_MODEL = 1024
SEQ = 4096
CHUNK = 64
N_HEADS = 8
QK_NOPE = 64
QK_ROPE = 32
QK_DIM = QK_NOPE + QK_ROPE
V_DIM = 64
Q_LORA = 384
KV_LORA = 256
ATTN_W = N_HEADS * V_DIM
CONV_W = D_MODEL - ATTN_W
CONV_K = 31
D_FF = 2816
ROPE_THETA = 10000.0
EPS = 1e-6
NEG = -1e30

LANES = 128
HEAD_SLAB = LANES
HALO = 32
VMEM_LIMIT = 56 * 1024 * 1024

TM_FFN = 512
TM_PROJ = 256
TM_MIX = 256
TQ = 256
TK = 256
CONV_ROWS = 32
ATTN_UNROLL = 4
MIX_PARTS = 2
SUBLANES = 8
V_ROWS = 80
LOG2E = 1.4426950408889634
F_CHUNKS = ((0, 1536), (1536, D_FF))

BF16 = jnp.bfloat16
F32 = jnp.float32


def _rms(x, g):
    y = x * lax.rsqrt(jnp.mean(x * x, axis=-1, keepdims=True) + EPS)
    return y * g


def _dot(a, b):
    return jnp.dot(a, b, preferred_element_type=F32)


def _const_spec(shape):
    return pl.BlockSpec(shape, lambda *_: (0,) * len(shape), pipeline_mode=pl.Buffered(1))


def _params(n_axes):
    return pltpu.CompilerParams(dimension_semantics=("arbitrary",) * n_axes, vmem_limit_bytes=VMEM_LIMIT)


def _ffn_kernel(x_ref, g_ref, wg_ref, wu_ref, wd_ref, *rest, post_norm):
    if post_norm:
        pg_ref, o_ref = rest
    else:
        (o_ref,) = rest
    x = x_ref[...]
    h = _rms(x, g_ref[...]).astype(BF16)
    y = jnp.zeros_like(x)
    for lo, hi in F_CHUNKS:
        gate = _dot(h, wg_ref[:, lo:hi])
        up = _dot(h, wu_ref[:, lo:hi])
        a = (gate * jax.nn.sigmoid(gate) * up).astype(BF16)
        y = y + _dot(a, wd_ref[lo:hi, :])
    out = x + 0.5 * y
    if post_norm:
        out = _rms(out, pg_ref[...])
    o_ref[...] = out


def _ffn(x2d, g, wg, wu, wd, post_g=None):
    n, d = x2d.shape
    row = pl.BlockSpec((TM_FFN, d), lambda i: (i, 0))
    in_specs = [row, _const_spec((1, d)), _const_spec(wg.shape), _const_spec(wu.shape), _const_spec(wd.shape)]
    args = [x2d, g.reshape(1, d), wg, wu, wd]
    if post_g is not None:
        in_specs.append(_const_spec((1, d)))
        args.append(post_g.reshape(1, d))
    return pl.pallas_call(
        functools.partial(_ffn_kernel, post_norm=post_g is not None),
        grid=(n // TM_FFN,),
        in_specs=in_specs,
        out_specs=row,
        out_shape=jax.ShapeDtypeStruct((n, d), F32),
        compiler_params=_params(1),
        name="ffn_post" if post_g is not None else "ffn",
    )(*args)


def _rope(x, c, sa, sb):
    return x * c + pltpu.roll(x, HEAD_SLAB - QK_ROPE // 2, 1) * sa + pltpu.roll(x, QK_ROPE // 2, 1) * sb


def _proj_kernel(x_ref, g_ref, wqt_ref, wkv_ref, wpe_ref, wcv_ref, qlg_ref, wuqt_ref, kvlg_ref, wk_ref, wvt_ref,
                 qg_ref, kg_ref, c_ref, sa_ref, sb_ref, cos_ref, sin_ref, qt_ref, k_ref, vt_ref, u_ref):
    h = _rms(x_ref[...], g_ref[...]).astype(BF16)
    c, sa, sb = c_ref[...], sa_ref[...], sb_ref[...]

    nt = (((1,), (1,)), ((), ()))
    cqt_raw = lax.dot_general(wqt_ref[...], h, nt, preferred_element_type=F32)
    ckv_raw = _dot(h, wkv_ref[...])
    kpe = _dot(h, wpe_ref[...])
    cv_a = _dot(h, wcv_ref[:, :CONV_W])
    cqt = cqt_raw * lax.rsqrt(jnp.mean(cqt_raw * cqt_raw, axis=0, keepdims=True) + EPS) * qlg_ref[...]
    ckv = _rms(ckv_raw, kvlg_ref[...]).astype(BF16)
    qft = _dot(wuqt_ref[...], cqt.astype(BF16))
    kf = _dot(ckv, wk_ref[...])
    vt = lax.dot_general(wvt_ref[...], ckv, nt, preferred_element_type=F32)
    cv_g = _dot(h, wcv_ref[:, CONV_W:])

    qg, cos_t, sin_t = qg_ref[...], cos_ref[...], sin_ref[...]
    half = QK_ROPE // 2
    r1, r2 = QK_NOPE, QK_NOPE + half
    slab_pad = jnp.zeros((HEAD_SLAB - QK_DIM, qft.shape[1]), BF16)
    for hd in range(N_HEADS):
        base = hd * HEAD_SLAB
        qh = qft[base:base + QK_DIM, :]
        qn = qh * lax.rsqrt(jnp.sum(qh * qh, axis=0, keepdims=True) / QK_DIM + EPS) * qg
        x1, x2 = qn[r1:r2], qn[r2:QK_DIM]
        qt_ref[base:base + r1, :] = qn[:r1].astype(BF16)
        qt_ref[base + r1:base + r2, :] = (x1 * cos_t - x2 * sin_t).astype(BF16)
        qt_ref[base + r2:base + QK_DIM, :] = (x1 * sin_t + x2 * cos_t).astype(BF16)
        qt_ref[base + QK_DIM:base + HEAD_SLAB, :] = slab_pad

    pad_rows = V_ROWS - V_DIM
    ones_row = jnp.where(lax.broadcasted_iota(jnp.int32, (pad_rows, vt.shape[1]), 0) == 0, 1.0, 0.0).astype(BF16)
    for hd in range(N_HEADS):
        vt_ref[hd, 0:V_DIM, :] = vt[hd * V_DIM:(hd + 1) * V_DIM, :].astype(BF16)
        vt_ref[hd, V_DIM:V_ROWS, :] = ones_row
    kg = kg_ref[...]
    pe_ss = jnp.sum(kpe * kpe, axis=-1, keepdims=True)
    pe_rot = _rope(kpe * kg, c, sa, sb)
    for hd in range(N_HEADS):
        sl = slice(hd * HEAD_SLAB, (hd + 1) * HEAD_SLAB)
        kh = kf[:, sl]
        rs = lax.rsqrt((jnp.sum(kh * kh, axis=-1, keepdims=True) + pe_ss) / QK_DIM + EPS)
        k_ref[:, sl] = ((kh * kg + pe_rot) * rs).astype(BF16)

    u_ref[...] = cv_a * jax.nn.sigmoid(cv_g)


def _proj(x2d, g, wqt, wkv, wpe, wcv, qlg, wuqt, kvlg, wk, wvt, qg, kg, tabs, tabs_t):
    n, d = x2d.shape
    tm = TM_PROJ
    assert tm == TK
    s_blocks = SEQ // tm
    batch = n // SEQ
    hs = N_HEADS * HEAD_SLAB
    row = lambda w: pl.BlockSpec((tm, w), lambda i: (i, 0))
    tab = pl.BlockSpec((tm, HEAD_SLAB), lambda i: (i % s_blocks, 0))
    tab_t = pl.BlockSpec((QK_ROPE // 2, tm), lambda i: (0, i % s_blocks))
    qt_spec = pl.BlockSpec((None, hs, tm), lambda i: (i // s_blocks, 0, i % s_blocks))
    vt_spec = pl.BlockSpec((None, N_HEADS, None, V_ROWS, TK), lambda i: (i // s_blocks, 0, i % s_blocks, 0, 0))
    consts = [g, wqt, wkv, wpe, wcv, qlg, wuqt, kvlg, wk, wvt, qg, kg]
    in_specs = [row(d)] + [_const_spec(a.shape) for a in consts] + [tab] * 3 + [tab_t] * 2
    return pl.pallas_call(
        _proj_kernel,
        grid=(n // tm,),
        in_specs=in_specs,
        out_specs=[qt_spec, row(hs), vt_spec, row(CONV_W)],
        out_shape=[jax.ShapeDtypeStruct((batch, hs, SEQ), BF16), jax.ShapeDtypeStruct((n, hs), BF16),
                   jax.ShapeDtypeStruct((batch, N_HEADS, s_blocks, V_ROWS, TK), BF16),
                   jax.ShapeDtypeStruct((n, CONV_W), F32)],
        compiler_params=_params(1),
        name="proj",
    )(x2d, *consts, *tabs, *tabs_t)


def _attn_kernel(qt_ref, k_ref, vt_ref, o_ref, sta_ref, stb_ref, m_ref, acc_ref):
    qi = pl.program_id(1)
    m_ref[...] = jnp.full(m_ref.shape, NEG, F32)
    acc_ref[...] = jnp.zeros(acc_ref.shape, F32)
    key_chunk = lax.broadcasted_iota(jnp.int32, (TK, TQ), 0) // CHUNK
    qry_chunk = lax.broadcasted_iota(jnp.int32, (TK, TQ), 1) // CHUNK
    allowed = key_chunk <= qry_chunk

    def scores(kj, hd, dst_ref):
        k0 = pl.multiple_of(kj * TK, TK)
        sl = slice(hd * HEAD_SLAB, (hd + 1) * HEAD_SLAB)
        dst_ref[hd] = _dot(k_ref[pl.ds(k0, TK), sl], qt_ref[sl, :])

    def softmax_pv(kj, hd, src_ref, masked):
        st = src_ref[hd]
        if masked:
            st = jnp.where(allowed, st, NEG)
        m_old = m_ref[hd]
        m_new = jnp.maximum(m_old, jnp.max(st, axis=0, keepdims=True))
        p = jnp.exp2(st - m_new).astype(BF16)
        acc_ref[hd] = jnp.exp2(m_old - m_new) * acc_ref[hd] + _dot(vt_ref[hd, kj], p)
        m_ref[hd] = m_new

    def stage(kj, cur_ref, nxt_ref, masked=False):
        for hd in range(N_HEADS):
            if nxt_ref is not None:
                scores(kj + 1, hd, nxt_ref)
            softmax_pv(kj, hd, cur_ref, masked)

    for hd in range(N_HEADS):
        scores(0, hd, sta_ref)

    bufs = (sta_ref, stb_ref)

    def body(t, carry):
        for s in range(ATTN_UNROLL):
            stage(ATTN_UNROLL * t + s, bufs[s % 2], bufs[(s + 1) % 2])
        return carry

    trips = qi // ATTN_UNROLL
    lax.fori_loop(0, trips, body, 0)
    rest = qi - trips * ATTN_UNROLL
    for r in range(ATTN_UNROLL):
        @pl.when(rest == r)
        def _(r=r):
            for s in range(r):
                stage(qi - r + s, bufs[s % 2], bufs[(s + 1) % 2])
            stage(qi, bufs[r % 2], None, masked=True)

    for pair in range(N_HEADS // 2):
        halves = []
        for hd in (2 * pair, 2 * pair + 1):
            acc = acc_ref[hd]
            halves.append(acc[:V_DIM] / acc[V_DIM:V_DIM + 1])
        pair_t = jnp.concatenate(halves, axis=0)
        o_ref[:, pair * 2 * V_DIM:(pair + 1) * 2 * V_DIM] = pair_t.T.astype(BF16)


def _attn(qt, k, vt, batch):
    hs = N_HEADS * HEAD_SLAB
    k3 = k.reshape(batch, SEQ, hs)
    out = pl.pallas_call(
        _attn_kernel,
        grid=(batch, SEQ // TQ),
        in_specs=[pl.BlockSpec((None, hs, TQ), lambda b, qi: (b, 0, qi)),
                  pl.BlockSpec((None, SEQ, hs), lambda b, qi: (b, 0, 0)),
                  pl.BlockSpec((None, N_HEADS, SEQ // TK, V_ROWS, TK), lambda b, qi: (b, 0, 0, 0, 0))],
        out_specs=pl.BlockSpec((None, TQ, ATTN_W), lambda b, qi: (b, qi, 0)),
        out_shape=jax.ShapeDtypeStruct((batch, SEQ, ATTN_W), BF16),
        scratch_shapes=[pltpu.VMEM((N_HEADS, TK, TQ), F32), pltpu.VMEM((N_HEADS, TK, TQ), F32),
                        pltpu.VMEM((N_HEADS, 1, TQ), F32),
                        pltpu.VMEM((N_HEADS, V_ROWS, TQ), F32)],
        compiler_params=_params(2),
        name="attn",
    )(qt, k3, vt)
    return out.reshape(batch * SEQ, ATTN_W)


def _mix_out_kernel(uprev_ref, u_ref, attn_ref, x_ref, cw_ref, cb_ref, lg_ref, lb_ref, woa_ref, woc_ref,
                    o_ref, ext_ref, conv_ref):
    tm = TM_MIX
    first = (pl.program_id(0) % (SEQ // tm)) == 0
    ext_ref[0, 0:HALO, :] = jnp.where(first, 0.0, uprev_ref[...])
    ext_ref[0, HALO:HALO + tm, :] = u_ref[...]
    span = HALO + tm - SUBLANES
    for b in range(1, SUBLANES):
        ext_ref[b, 0:span, :] = ext_ref[0, b:b + span, :]

    o_ref[...] = x_ref[...] + _dot(attn_ref[...], woa_ref[...])

    cb, lg, lb = cb_ref[...], lg_ref[...], lb_ref[...]
    base = HALO - (CONV_K - 1)
    groups = CONV_ROWS // SUBLANES
    part = tm // MIX_PARTS
    for p0 in range(0, tm, part):
        for r in range(p0, p0 + part, CONV_ROWS):
            accs = [jnp.zeros((SUBLANES, CONV_W), F32) for _ in range(groups)]
            for j in range(CONV_K):
                b = (base + j) % SUBLANES
                a = r + base + j - b
                w = cw_ref[j]
                for g in range(groups):
                    accs[g] = accs[g] + w * ext_ref[b, a + g * SUBLANES:a + (g + 1) * SUBLANES, :]
            y = jnp.concatenate(accs, axis=0) + cb
            mu = jnp.mean(y, axis=-1, keepdims=True)
            yc = y - mu
            z = yc * lax.rsqrt(jnp.mean(yc * yc, axis=-1, keepdims=True) + EPS) * lg + lb
            conv_ref[r:r + CONV_ROWS, :] = (z * jax.nn.sigmoid(z)).astype(BF16)
        o_ref[p0:p0 + part, :] += _dot(conv_ref[p0:p0 + part, :], woc_ref[...])


def _mix_out(u, attn, x2d, cw, cb, lg, lb, woa, woc):
    n, d = x2d.shape
    tm = TM_MIX
    row = lambda w: pl.BlockSpec((tm, w), lambda i: (i, 0))
    halo = pl.BlockSpec((HALO, CONV_W), lambda i: (jnp.maximum(i * (tm // HALO) - 1, 0), 0))
    consts = [cw, cb, lg, lb, woa, woc]
    return pl.pallas_call(
        _mix_out_kernel,
        grid=(n // tm,),
        in_specs=[halo, row(CONV_W), row(ATTN_W), row(d)] + [_const_spec(a.shape) for a in consts],
        out_specs=row(d),
        out_shape=jax.ShapeDtypeStruct((n, d), F32),
        scratch_shapes=[pltpu.VMEM((SUBLANES, HALO + tm, CONV_W), F32), pltpu.VMEM((tm, CONV_W), BF16)],
        compiler_params=_params(1),
        name="mix_out",
    )(u, u, attn, x2d, *consts)


def _rope_slab_tables(seq_len):
    pos = jnp.arange(seq_len, dtype=F32)
    inv_freq = 1.0 / (ROPE_THETA ** (jnp.arange(0, QK_ROPE, 2, dtype=F32) / QK_ROPE))
    ang = pos[:, None] * inv_freq[None, :]
    cos, sin = jnp.cos(ang), jnp.sin(ang)
    half = QK_ROPE // 2
    ones = jnp.ones((seq_len, QK_NOPE), F32)
    z = lambda w: jnp.zeros((seq_len, w), F32)
    tail = HEAD_SLAB - QK_DIM
    c_tab = jnp.concatenate([ones, cos, cos, z(tail)], axis=1)
    sa_tab = jnp.concatenate([z(QK_NOPE), -sin, z(half), z(tail)], axis=1)
    sb_tab = jnp.concatenate([z(QK_NOPE), z(half), sin, z(tail)], axis=1)
    return (c_tab, sa_tab, sb_tab), (cos.T, sin.T)


def _head_slabs(w, width, offset=0):
    kdim = w.shape[0]
    w3 = w.reshape(kdim, N_HEADS, width)
    w3 = jnp.pad(w3, ((0, 0), (0, 0), (offset, HEAD_SLAB - width - offset)))
    return w3.reshape(kdim, N_HEADS * HEAD_SLAB)


def _slab_vec(g, offset=0):
    return jnp.pad(g, (offset, HEAD_SLAB - g.shape[0] - offset)).reshape(1, HEAD_SLAB)


def kernel(x, ffn1_norm, ffn1_w_gate, ffn1_w_up, ffn1_w_down, mix_norm, w_in, q_latent_norm, w_uq,
           kv_latent_norm, w_ukv, q_norm, k_norm, conv_w, conv_b, conv_ln_g, conv_ln_b, w_out,
           ffn2_norm, ffn2_w_gate, ffn2_w_up, ffn2_w_down, post_norm):
    batch, seq, d = x.shape
    assert (seq, d) == (SEQ, D_MODEL)
    depth = ffn1_norm.shape[0]
    tabs, tabs_t = _rope_slab_tables(seq)
    o_kv, o_pe, o_cv = Q_LORA, Q_LORA + KV_LORA, Q_LORA + KV_LORA + QK_ROPE
    bf = lambda a: a.astype(BF16)
    row = lambda a: a.reshape(1, -1)
    lanes = lambda a: jnp.broadcast_to(a[:, None], (a.shape[0], TM_PROJ))

    x2d = x.reshape(batch * seq, d)
    for l in range(depth):
        x2d = _ffn(x2d, ffn1_norm[l], bf(ffn1_w_gate[l]), bf(ffn1_w_up[l]), bf(ffn1_w_down[l]))

        wi = w_in[l]
        wpe = jnp.pad(wi[:, o_pe:o_cv], ((0, 0), (QK_NOPE, HEAD_SLAB - QK_DIM)))
        ukv = w_ukv[l].reshape(KV_LORA, N_HEADS, QK_NOPE + V_DIM)
        wk = _head_slabs(ukv[:, :, :QK_NOPE].reshape(KV_LORA, N_HEADS * QK_NOPE), QK_NOPE)
        wvt = ukv[:, :, QK_NOPE:].reshape(KV_LORA, ATTN_W).T
        qg = lanes(q_norm[l] * (QK_DIM ** -0.5 * LOG2E))
        qt, k, vt, u = _proj(
            x2d, row(mix_norm[l]), bf(wi[:, :o_kv].T), bf(wi[:, o_kv:o_pe]), bf(wpe), bf(wi[:, o_cv:]),
            lanes(q_latent_norm[l]), bf(_head_slabs(w_uq[l], QK_DIM).T), row(kv_latent_norm[l]), bf(wk), bf(wvt),
            qg, _slab_vec(k_norm[l]), tabs, tabs_t)

        attn = _attn(qt, k, vt, batch)

        wo = w_out[l]
        cw = jnp.broadcast_to(conv_w[l][:, None, :], (CONV_K, SUBLANES, CONV_W))
        x2d = _mix_out(u, attn, x2d, cw, row(conv_b[l]), row(conv_ln_g[l]), row(conv_ln_b[l]),
                       bf(wo[:ATTN_W]), bf(wo[ATTN_W:]))

        x2d = _ffn(x2d, ffn2_norm[l], bf(ffn2_w_gate[l]), bf(ffn2_w_up[l]), bf(ffn2_w_down[l]),
                   post_g=post_norm[l])
    return x2d.reshape(batch, seq, d)
```

```python
import functools

import jax
import jax.numpy as jnp
from jax import lax
from jax.experimental import pallas as pl
from jax.experimental.pallas import tpu as pltpu

D_MODEL = 1024
SEQ = 4096
CHUNK = 64
N_HEADS = 8
QK_NOPE = 64
QK_ROPE = 32
QK_DIM = QK_NOPE + QK_ROPE
V_DIM = 64
Q_LORA = 384
KV_LORA = 256
ATTN_W = N_HEADS * V_DIM
CONV_W = D_MODEL - ATTN_W
CONV_K = 31
D_FF = 2816
ROPE_THETA = 10000.0
EPS = 1e-6
NEG = -1e30

LANES = 128
HEAD_SLAB = LANES
HALO = 32
VMEM_LIMIT = 56 * 1024 * 1024

TM_FFN = 512
TM_PROJ = 256
TM_MIX = 512
TQ = 512
TK = 256
DIAG_BLOCKS = TQ // TK
CONV_ROWS = 32
ATTN_UNROLL = 4
MIX_PARTS = 2
CAST_STEPS = 4
SUBLANES = 8
V_ROWS = 80
LOG2E = 1.4426950408889634
F_CHUNKS = ((0, 1536), (1536, D_FF))

BF16 = jnp.bfloat16
F32 = jnp.float32


def _rms(x, g):
    y = x * lax.rsqrt(jnp.mean(x * x, axis=-1, keepdims=True) + EPS)
    return y * g


def _dot(a, b):
    return jnp.dot(a, b, preferred_element_type=F32)


def _const_spec(shape):
    return pl.BlockSpec(shape, lambda *_: (0,) * len(shape), pipeline_mode=pl.Buffered(1))


def _params(n_axes):
    return pltpu.CompilerParams(dimension_semantics=("arbitrary",) * n_axes, vmem_limit_bytes=VMEM_LIMIT)


def _half_step_ffn(x, g, wg_ref, wu_ref, wd_ref):
    h = _rms(x, g).astype(BF16)
    y = jnp.zeros_like(x)
    for lo, hi in F_CHUNKS:
        gate = _dot(h, wg_ref[:, lo:hi])
        up = _dot(h, wu_ref[:, lo:hi])
        a = (gate * jax.nn.sigmoid(gate) * up).astype(BF16)
        y = y + _dot(a, wd_ref[lo:hi, :])
    return x + 0.5 * y


def _ffn_kernel(x_ref, g_ref, wg_ref, wu_ref, wd_ref, o_ref):
    o_ref[...] = _half_step_ffn(x_ref[...], g_ref[...], wg_ref, wu_ref, wd_ref)


def _ffn(x2d, g, wg, wu, wd):
    n, d = x2d.shape
    row = pl.BlockSpec((TM_FFN, d), lambda i: (i, 0))
    return pl.pallas_call(
        _ffn_kernel,
        grid=(n // TM_FFN,),
        in_specs=[row, _const_spec((1, d)), _const_spec(wg.shape), _const_spec(wu.shape), _const_spec(wd.shape)],
        out_specs=row,
        out_shape=jax.ShapeDtypeStruct((n, d), F32),
        compiler_params=_params(1),
        name="ffn",
    )(x2d, g.reshape(1, d), wg, wu, wd)


def _cast_kernel(*refs):
    n = len(refs) // 2
    for src, dst in zip(refs[:n], refs[n:]):
        dst[...] = src[...].astype(BF16)


def _ffn_weights_bf16(layer, *stacked):
    specs, out_specs, out_shapes = [], [], []
    for w in stacked:
        _, rows, cols = w.shape
        blk = rows // CAST_STEPS
        specs.append(pl.BlockSpec((None, blk, cols), lambda i, layer=layer: (layer, i, 0)))
        out_specs.append(pl.BlockSpec((blk, cols), lambda i: (i, 0)))
        out_shapes.append(jax.ShapeDtypeStruct((rows, cols), BF16))
    return pl.pallas_call(
        _cast_kernel,
        grid=(CAST_STEPS,),
        in_specs=specs,
        out_specs=out_specs,
        out_shape=out_shapes,
        compiler_params=_params(1),
        name="cast_w",
    )(*stacked)


def _rope(x, c, sa, sb):
    return x * c + pltpu.roll(x, HEAD_SLAB - QK_ROPE // 2, 1) * sa + pltpu.roll(x, QK_ROPE // 2, 1) * sb


def _proj_kernel(x_ref, g_ref, wqt_ref, wkv_ref, wpe_ref, wcv_ref, qlg_ref, wuqt_ref, kvlg_ref, wk_ref, wvt_ref,
                 qg_ref, kg_ref, c_ref, sa_ref, sb_ref, cos_ref, sin_ref, qt_ref, k_ref, vt_ref, u_ref):
    h = _rms(x_ref[...], g_ref[...]).astype(BF16)
    c, sa, sb = c_ref[...], sa_ref[...], sb_ref[...]

    nt = (((1,), (1,)), ((), ()))
    cqt_raw = lax.dot_general(wqt_ref[...], h, nt, preferred_element_type=F32)
    ckv_raw = _dot(h, wkv_ref[...])
    kpe = _dot(h, wpe_ref[...])
    cv_a = _dot(h, wcv_ref[:, :CONV_W])
    cqt = cqt_raw * lax.rsqrt(jnp.mean(cqt_raw * cqt_raw, axis=0, keepdims=True) + EPS) * qlg_ref[...]
    ckv = _rms(ckv_raw, kvlg_ref[...]).astype(BF16)
    qft = _dot(wuqt_ref[...], cqt.astype(BF16))
    kf = _dot(ckv, wk_ref[...])
    vt = lax.dot_general(wvt_ref[...], ckv, nt, preferred_element_type=F32)
    cv_g = _dot(h, wcv_ref[:, CONV_W:])

    qg, cos_t, sin_t = qg_ref[...], cos_ref[...], sin_ref[...]
    half = QK_ROPE // 2
    r1, r2 = QK_NOPE, QK_NOPE + half
    slab_pad = jnp.zeros((HEAD_SLAB - QK_DIM, qft.shape[1]), BF16)
    for hd in range(N_HEADS):
        base = hd * HEAD_SLAB
        qh = qft[base:base + QK_DIM, :]
        qn = qh * lax.rsqrt(jnp.sum(qh * qh, axis=0, keepdims=True) / QK_DIM + EPS) * qg
        x1, x2 = qn[r1:r2], qn[r2:QK_DIM]
        qt_ref[base:base + r1, :] = qn[:r1].astype(BF16)
        qt_ref[base + r1:base + r2, :] = (x1 * cos_t - x2 * sin_t).astype(BF16)
        qt_ref[base + r2:base + QK_DIM, :] = (x1 * sin_t + x2 * cos_t).astype(BF16)
        qt_ref[base + QK_DIM:base + HEAD_SLAB, :] = slab_pad

    pad_rows = V_ROWS - V_DIM
    ones_row = jnp.where(lax.broadcasted_iota(jnp.int32, (pad_rows, vt.shape[1]), 0) == 0, 1.0, 0.0).astype(BF16)
    for hd in range(N_HEADS):
        vt_ref[hd, 0:V_DIM, :] = vt[hd * V_DIM:(hd + 1) * V_DIM, :].astype(BF16)
        vt_ref[hd, V_DIM:V_ROWS, :] = ones_row
    kg = kg_ref[...]
    pe_ss = jnp.sum(kpe * kpe, axis=-1, keepdims=True)
    pe_rot = _rope(kpe * kg, c, sa, sb)
    for hd in range(N_HEADS):
        sl = slice(hd * HEAD_SLAB, (hd + 1) * HEAD_SLAB)
        kh = kf[:, sl]
        rs = lax.rsqrt((jnp.sum(kh * kh, axis=-1, keepdims=True) + pe_ss) / QK_DIM + EPS)
        k_ref[:, sl] = ((kh * kg + pe_rot) * rs).astype(BF16)

    u_ref[...] = cv_a * jax.nn.sigmoid(cv_g)


def _proj(x2d, g, wqt, wkv, wpe, wcv, qlg, wuqt, kvlg, wk, wvt, qg, kg, tabs, tabs_t):
    n, d = x2d.shape
    tm = TM_PROJ
    assert tm == TK
    s_blocks = SEQ // tm
    batch = n // SEQ
    hs = N_HEADS * HEAD_SLAB
    row = lambda w: pl.BlockSpec((tm, w), lambda i: (i, 0))
    tab = pl.BlockSpec((tm, HEAD_SLAB), lambda i: (i % s_blocks, 0))
    tab_t = pl.BlockSpec((QK_ROPE // 2, tm), lambda i: (0, i % s_blocks))
    qt_spec = pl.BlockSpec((None, hs, tm), lambda i: (i // s_blocks, 0, i % s_blocks))
    vt_spec = pl.BlockSpec((None, N_HEADS, None, V_ROWS, TK), lambda i: (i // s_blocks, 0, i % s_blocks, 0, 0))
    consts = [g, wqt, wkv, wpe, wcv, qlg, wuqt, kvlg, wk, wvt, qg, kg]
    in_specs = [row(d)] + [_const_spec(a.shape) for a in consts] + [tab] * 3 + [tab_t] * 2
    return pl.pallas_call(
        _proj_kernel,
        grid=(n // tm,),
        in_specs=in_specs,
        out_specs=[qt_spec, row(hs), vt_spec, row(CONV_W)],
        out_shape=[jax.ShapeDtypeStruct((batch, hs, SEQ), BF16), jax.ShapeDtypeStruct((n, hs), BF16),
                   jax.ShapeDtypeStruct((batch, N_HEADS, s_blocks, V_ROWS, TK), BF16),
                   jax.ShapeDtypeStruct((n, CONV_W), F32)],
        compiler_params=_params(1),
        name="proj",
    )(x2d, *consts, *tabs, *tabs_t)


def _attn_kernel(qt_ref, qt_next_ref, k_ref, vt_ref, o_ref, sta_ref, stb_ref, m_ref, acc_ref):
    qi = pl.program_id(1)
    m_ref[...] = jnp.full(m_ref.shape, NEG, F32)
    acc_ref[...] = jnp.zeros(acc_ref.shape, F32)
    key_pos = lax.broadcasted_iota(jnp.int32, (TK, TQ), 0)
    qry_chunk = lax.broadcasted_iota(jnp.int32, (TK, TQ), 1) // CHUNK
    diag_masks = [(key_pos + d * TK) // CHUNK <= qry_chunk for d in range(DIAG_BLOCKS)]

    def scores(kj, hd, dst_ref, q_ref=qt_ref):
        k0 = pl.multiple_of(kj * TK, TK)
        sl = slice(hd * HEAD_SLAB, (hd + 1) * HEAD_SLAB)
        dst_ref[hd] = _dot(k_ref[pl.ds(k0, TK), sl], q_ref[sl, :])

    def softmax_pv(kj, hd, src_ref, mask):
        st = src_ref[hd]
        if mask is not None:
            st = jnp.where(mask, st, NEG)
        m_old = m_ref[hd]
        m_new = jnp.maximum(m_old, jnp.max(st, axis=0, keepdims=True))
        p = jnp.exp2(st - m_new).astype(BF16)
        acc_ref[hd] = jnp.exp2(m_old - m_new) * acc_ref[hd] + _dot(vt_ref[hd, kj], p)
        m_ref[hd] = m_new

    def stage(kj, cur_ref, nxt_ref, mask=None, last=False):
        for hd in range(N_HEADS):
            if last:
                scores(0, hd, nxt_ref, qt_next_ref)
            else:
                scores(kj + 1, hd, nxt_ref)
            softmax_pv(kj, hd, cur_ref, mask)

    @pl.when(qi == 0)
    def _():
        for hd in range(N_HEADS):
            scores(0, hd, sta_ref)

    bufs = (sta_ref, stb_ref)

    def body(t, carry):
        for s in range(ATTN_UNROLL):
            stage(ATTN_UNROLL * t + s, bufs[s % 2], bufs[(s + 1) % 2])
        return carry

    n_free = qi * DIAG_BLOCKS
    trips = n_free // ATTN_UNROLL
    lax.fori_loop(0, trips, body, 0)
    rest = n_free - trips * ATTN_UNROLL
    for r in range(0, ATTN_UNROLL, DIAG_BLOCKS):
        @pl.when(rest == r)
        def _(r=r):
            for s in range(r + DIAG_BLOCKS):
                stage(n_free - r + s, bufs[s % 2], bufs[(s + 1) % 2],
                      mask=diag_masks[s - r] if s >= r else None, last=s == r + DIAG_BLOCKS - 1)

    for pair in range(N_HEADS // 2):
        halves = []
        for hd in (2 * pair, 2 * pair + 1):
            acc = acc_ref[hd]
            halves.append(acc[:V_DIM] / acc[V_DIM:V_DIM + 1])
        pair_t = jnp.concatenate(halves, axis=0)
        o_ref[:, pair * 2 * V_DIM:(pair + 1) * 2 * V_DIM] = pair_t.T.astype(BF16)


def _attn(qt, k, vt, batch):
    hs = N_HEADS * HEAD_SLAB
    k3 = k.reshape(batch, SEQ, hs)
    assert DIAG_BLOCKS % 2 == 0 and ATTN_UNROLL % DIAG_BLOCKS == 0
    last = SEQ // TQ - 1
    out = pl.pallas_call(
        _attn_kernel,
        grid=(batch, SEQ // TQ),
        in_specs=[pl.BlockSpec((None, hs, TQ), lambda b, qi: (b, 0, qi)),
                  pl.BlockSpec((None, hs, TQ), lambda b, qi: (b, 0, jnp.minimum(qi + 1, last))),
                  pl.BlockSpec((None, SEQ, hs), lambda b, qi: (b, 0, 0)),
                  pl.BlockSpec((None, N_HEADS, SEQ // TK, V_ROWS, TK), lambda b, qi: (b, 0, 0, 0, 0))],
        out_specs=pl.BlockSpec((None, TQ, ATTN_W), lambda b, qi: (b, qi, 0)),
        out_shape=jax.ShapeDtypeStruct((batch, SEQ, ATTN_W), BF16),
        scratch_shapes=[pltpu.VMEM((N_HEADS, TK, TQ), F32), pltpu.VMEM((N_HEADS, TK, TQ), F32),
                        pltpu.VMEM((N_HEADS, 1, TQ), F32),
                        pltpu.VMEM((N_HEADS, V_ROWS, TQ), F32)],
        compiler_params=_params(2),
        name="attn",
    )(qt, qt, k3, vt)
    return out.reshape(batch * SEQ, ATTN_W)


def _conv_module(halo, u, cw_ref, cb, lg, lb, ext_ref, dst_ref):
    rows = u.shape[0]
    ext_ref[0, 0:HALO, :] = halo
    ext_ref[0, HALO:HALO + rows, :] = u
    span = HALO + rows - SUBLANES
    for b in range(1, SUBLANES):
        ext_ref[b, 0:span, :] = ext_ref[0, b:b + span, :]

    base = HALO - (CONV_K - 1)
    groups = CONV_ROWS // SUBLANES
    for r in range(0, rows, CONV_ROWS):
        accs = [jnp.zeros((SUBLANES, CONV_W), F32) for _ in range(groups)]
        for j in range(CONV_K):
            b = (base + j) % SUBLANES
            a = r + base + j - b
            w = cw_ref[j]
            for g in range(groups):
                accs[g] = accs[g] + w * ext_ref[b, a + g * SUBLANES:a + (g + 1) * SUBLANES, :]
        y = jnp.concatenate(accs, axis=0) + cb
        mu = jnp.mean(y, axis=-1, keepdims=True)
        yc = y - mu
        z = yc * lax.rsqrt(jnp.mean(yc * yc, axis=-1, keepdims=True) + EPS) * lg + lb
        dst_ref[r:r + CONV_ROWS, :] = (z * jax.nn.sigmoid(z)).astype(BF16)


def _mix_ffn_kernel(halo_ref, u_ref, attn_ref, x_ref, cw_ref, cb_ref, lg_ref, lb_ref, woa_ref, woc_ref,
                    g_ref, wg_ref, wu_ref, wd_ref, pg_ref, o_ref, ext_ref, *conv_refs):
    first = (pl.program_id(0) % (SEQ // TM_MIX)) == 0
    cb, lg, lb = cb_ref[...], lg_ref[...], lb_ref[...]
    part = TM_MIX // MIX_PARTS
    for p, conv_ref in enumerate(conv_refs):
        rows = slice(p * part, (p + 1) * part)
        if p == 0:
            halo = jnp.where(first, 0.0, halo_ref[...])
        else:
            halo = u_ref[p * part - HALO:p * part, :]
        _conv_module(halo, u_ref[rows, :], cw_ref, cb, lg, lb, ext_ref, conv_ref)
        x = x_ref[rows, :] + _dot(attn_ref[rows, :], woa_ref[...]) + _dot(conv_ref[...], woc_ref[...])
        o_ref[rows, :] = _rms(_half_step_ffn(x, g_ref[...], wg_ref, wu_ref, wd_ref), pg_ref[...])


def _mix_ffn(u, attn, x2d, cw, cb, lg, lb, woa, woc, g, wg, wu, wd, pg):
    n, d = x2d.shape
    tm = TM_MIX
    part = tm // MIX_PARTS
    row = lambda w: pl.BlockSpec((tm, w), lambda i: (i, 0))
    halo = pl.BlockSpec((HALO, CONV_W), lambda i: (jnp.maximum(i * (tm // HALO) - 1, 0), 0))
    consts = [cw, cb, lg, lb, woa, woc, g.reshape(1, d), wg, wu, wd, pg.reshape(1, d)]
    return pl.pallas_call(
        _mix_ffn_kernel,
        grid=(n // tm,),
        in_specs=[halo, row(CONV_W), row(ATTN_W), row(d)] + [_const_spec(a.shape) for a in consts],
        out_specs=row(d),
        out_shape=jax.ShapeDtypeStruct((n, d), F32),
        scratch_shapes=[pltpu.VMEM((SUBLANES, HALO + part, CONV_W), F32)]
                       + [pltpu.VMEM((part, CONV_W), BF16)] * MIX_PARTS,
        compiler_params=_params(1),
        name="mix_ffn",
    )(u, u, attn, x2d, *consts)


def _rope_slab_tables(seq_len):
    pos = jnp.arange(seq_len, dtype=F32)
    inv_freq = 1.0 / (ROPE_THETA ** (jnp.arange(0, QK_ROPE, 2, dtype=F32) / QK_ROPE))
    ang = pos[:, None] * inv_freq[None, :]
    cos, sin = jnp.cos(ang), jnp.sin(ang)
    half = QK_ROPE // 2
    ones = jnp.ones((seq_len, QK_NOPE), F32)
    z = lambda w: jnp.zeros((seq_len, w), F32)
    tail = HEAD_SLAB - QK_DIM
    c_tab = jnp.concatenate([ones, cos, cos, z(tail)], axis=1)
    sa_tab = jnp.concatenate([z(QK_NOPE), -sin, z(half), z(tail)], axis=1)
    sb_tab = jnp.concatenate([z(QK_NOPE), z(half), sin, z(tail)], axis=1)
    return (c_tab, sa_tab, sb_tab), (cos.T, sin.T)


def _head_slabs(w, width, offset=0):
    kdim = w.shape[0]
    w3 = w.reshape(kdim, N_HEADS, width)
    w3 = jnp.pad(w3, ((0, 0), (0, 0), (offset, HEAD_SLAB - width - offset)))
    return w3.reshape(kdim, N_HEADS * HEAD_SLAB)


def _slab_vec(g, offset=0):
    return jnp.pad(g, (offset, HEAD_SLAB - g.shape[0] - offset)).reshape(1, HEAD_SLAB)


def kernel(x, ffn1_norm, ffn1_w_gate, ffn1_w_up, ffn1_w_down, mix_norm, w_in, q_latent_norm, w_uq,
           kv_latent_norm, w_ukv, q_norm, k_norm, conv_w, conv_b, conv_ln_g, conv_ln_b, w_out,
           ffn2_norm, ffn2_w_gate, ffn2_w_up, ffn2_w_down, post_norm):
    batch, seq, d = x.shape
    assert (seq, d) == (SEQ, D_MODEL)
    depth = ffn1_norm.shape[0]
    tabs, tabs_t = _rope_slab_tables(seq)
    o_kv, o_pe, o_cv = Q_LORA, Q_LORA + KV_LORA, Q_LORA + KV_LORA + QK_ROPE
    bf = lambda a: a.astype(BF16)
    row = lambda a: a.reshape(1, -1)
    lanes = lambda a: jnp.broadcast_to(a[:, None], (a.shape[0], TM_PROJ))

    x2d = x.reshape(batch * seq, d)
    for l in range(depth):
        x2d = _ffn(x2d, ffn1_norm[l], *_ffn_weights_bf16(l, ffn1_w_gate, ffn1_w_up, ffn1_w_down))

        wi = w_in[l]
        wpe = jnp.pad(wi[:, o_pe:o_cv], ((0, 0), (QK_NOPE, HEAD_SLAB - QK_DIM)))
        ukv = w_ukv[l].reshape(KV_LORA, N_HEADS, QK_NOPE + V_DIM)
        wk = _head_slabs(ukv[:, :, :QK_NOPE].reshape(KV_LORA, N_HEADS * QK_NOPE), QK_NOPE)
        wvt = ukv[:, :, QK_NOPE:].reshape(KV_LORA, ATTN_W).T
        qg = lanes(q_norm[l] * (QK_DIM ** -0.5 * LOG2E))
        qt, k, vt, u = _proj(
            x2d, row(mix_norm[l]), bf(wi[:, :o_kv].T), bf(wi[:, o_kv:o_pe]), bf(wpe), bf(wi[:, o_cv:]),
            lanes(q_latent_norm[l]), bf(_head_slabs(w_uq[l], QK_DIM).T), row(kv_latent_norm[l]), bf(wk), bf(wvt),
            qg, _slab_vec(k_norm[l]), tabs, tabs_t)

        attn = _attn(qt, k, vt, batch)

        wo = w_out[l]
        cw = jnp.broadcast_to(conv_w[l][:, None, :], (CONV_K, SUBLANES, CONV_W))
        x2d = _mix_ffn(u, attn, x2d, cw, row(conv_b[l]), row(conv_ln_g[l]), row(conv_ln_b[l]),
                       bf(wo[:ATTN_W]), bf(wo[ATTN_W:]),
                       ffn2_norm[l], *_ffn_weights_bf16(l, ffn2_w_gate, ffn2_w_up, ffn2_w_down), post_norm[l])
    return x2d.reshape(batch, seq, d)
```

```python
import jax
import jax.numpy as jnp
from jax import lax
from jax.experimental import pallas as pl
from jax.experimental.pallas import tpu as pltpu

D_MODEL = 1024
SEQ = 4096
CHUNK = 64
N_HEADS = 8
QK_NOPE = 64
QK_ROPE = 32
QK_DIM = QK_NOPE + QK_ROPE
V_DIM = 64
Q_LORA = 384
KV_LORA = 256
ATTN_W = N_HEADS * V_DIM
CONV_W = D_MODEL - ATTN_W
CONV_K = 31
D_FF = 2816
ROPE_THETA = 10000.0
EPS = 1e-6
NEG = -1e30

LANES = 128
SUBLANES = 8
HEAD_SLAB = LANES
HALO = 32
VMEM_LIMIT = 56 * 1024 * 1024

TM_FFN = 512
TM_PROJ = 512
TM_MIX = 512
TQ = 512
TK = 256
PROJ_PARTS = TM_PROJ // TK
MIX_PARTS = 2
DIAG_BLOCKS = TQ // TK
ATTN_UNROLL = 4
CONV_ROWS = 32
CAST_STEPS = 4
V_ROWS = 80
LOG2E = 1.4426950408889634
F_CHUNKS = ((0, 1536), (1536, D_FF))

BF16 = jnp.bfloat16
F32 = jnp.float32


def _rms(x, g):
    y = x * lax.rsqrt(jnp.mean(x * x, axis=-1, keepdims=True) + EPS)
    return y * g


def _dot(a, b):
    return jnp.dot(a, b, preferred_element_type=F32)


def _const_spec(shape):
    return pl.BlockSpec(shape, lambda *_: (0,) * len(shape), pipeline_mode=pl.Buffered(1))


def _params(n_axes):
    return pltpu.CompilerParams(dimension_semantics=("arbitrary",) * n_axes, vmem_limit_bytes=VMEM_LIMIT)


def _half_step_ffn(x, g, wg_ref, wu_ref, wd_ref):
    h = _rms(x, g).astype(BF16)
    y = jnp.zeros_like(x)
    for lo, hi in F_CHUNKS:
        gate = _dot(h, wg_ref[:, lo:hi])
        up = _dot(h, wu_ref[:, lo:hi])
        a = (gate * jax.nn.sigmoid(gate) * up).astype(BF16)
        y = y + _dot(a, wd_ref[lo:hi, :])
    return x + 0.5 * y


def _ffn_kernel(x_ref, g_ref, wg_ref, wu_ref, wd_ref, o_ref):
    o_ref[...] = _half_step_ffn(x_ref[...], g_ref[...], wg_ref, wu_ref, wd_ref)


def _ffn(x2d, g, wg, wu, wd):
    n, d = x2d.shape
    row = pl.BlockSpec((TM_FFN, d), lambda i: (i, 0))
    return pl.pallas_call(
        _ffn_kernel,
        grid=(n // TM_FFN,),
        in_specs=[row, _const_spec((1, d)), _const_spec(wg.shape), _const_spec(wu.shape), _const_spec(wd.shape)],
        out_specs=row,
        out_shape=jax.ShapeDtypeStruct((n, d), F32),
        compiler_params=_params(1),
        name="ffn",
    )(x2d, g.reshape(1, d), wg, wu, wd)


def _cast_kernel(*refs):
    n = len(refs) // 2
    for src, dst in zip(refs[:n], refs[n:]):
        dst[...] = src[...].astype(BF16)


def _ffn_weights_bf16(layer, *stacked):
    specs, out_specs, out_shapes = [], [], []
    for w in stacked:
        _, rows, cols = w.shape
        blk = rows // CAST_STEPS
        specs.append(pl.BlockSpec((None, blk, cols), lambda i, layer=layer: (layer, i, 0)))
        out_specs.append(pl.BlockSpec((blk, cols), lambda i: (i, 0)))
        out_shapes.append(jax.ShapeDtypeStruct((rows, cols), BF16))
    return pl.pallas_call(
        _cast_kernel,
        grid=(CAST_STEPS,),
        in_specs=specs,
        out_specs=out_specs,
        out_shape=out_shapes,
        compiler_params=_params(1),
        name="cast_w",
    )(*stacked)


def _rope(x, c, sa, sb):
    return x * c + pltpu.roll(x, HEAD_SLAB - QK_ROPE // 2, 1) * sa + pltpu.roll(x, QK_ROPE // 2, 1) * sb


def _proj_kernel(x_ref, g_ref, wqt_ref, wkv_ref, wpe_ref, wcv_ref, qlg_ref, wuqt_ref, kvlg_ref, wk_ref, wvt_ref,
                 qg_ref, kg_ref, c_ref, sa_ref, sb_ref, cos_ref, sin_ref, qt_ref, k_ref, vt_ref, u_ref):
    nt = (((1,), (1,)), ((), ()))
    half = QK_ROPE // 2
    r1, r2 = QK_NOPE, QK_NOPE + half
    qg, kg = qg_ref[...], kg_ref[...]
    slab_pad = jnp.zeros((HEAD_SLAB - QK_DIM, TK), BF16)
    ones_row = jnp.where(lax.broadcasted_iota(jnp.int32, (V_ROWS - V_DIM, TK), 0) == 0, 1.0, 0.0).astype(BF16)

    for p in range(PROJ_PARTS):
        rows = slice(p * TK, (p + 1) * TK)
        h = _rms(x_ref[rows, :], g_ref[...]).astype(BF16)
        c, sa, sb = c_ref[rows, :], sa_ref[rows, :], sb_ref[rows, :]
        cos_t, sin_t = cos_ref[:, rows], sin_ref[:, rows]

        cqt_raw = lax.dot_general(wqt_ref[...], h, nt, preferred_element_type=F32)
        ckv_raw = _dot(h, wkv_ref[...])
        kpe = _dot(h, wpe_ref[...])
        cv_a = _dot(h, wcv_ref[:, :CONV_W])
        cqt = cqt_raw * lax.rsqrt(jnp.mean(cqt_raw * cqt_raw, axis=0, keepdims=True) + EPS) * qlg_ref[...]
        ckv = _rms(ckv_raw, kvlg_ref[...]).astype(BF16)
        qft = _dot(wuqt_ref[...], cqt.astype(BF16))
        kf = _dot(ckv, wk_ref[...])
        vt = lax.dot_general(wvt_ref[...], ckv, nt, preferred_element_type=F32)
        cv_g = _dot(h, wcv_ref[:, CONV_W:])

        for hd in range(N_HEADS):
            base = hd * HEAD_SLAB
            qh = qft[base:base + QK_DIM, :]
            qn = qh * lax.rsqrt(jnp.sum(qh * qh, axis=0, keepdims=True) / QK_DIM + EPS) * qg
            x1, x2 = qn[r1:r2], qn[r2:QK_DIM]
            qt_ref[base:base + r1, rows] = qn[:r1].astype(BF16)
            qt_ref[base + r1:base + r2, rows] = (x1 * cos_t - x2 * sin_t).astype(BF16)
            qt_ref[base + r2:base + QK_DIM, rows] = (x1 * sin_t + x2 * cos_t).astype(BF16)
            qt_ref[base + QK_DIM:base + HEAD_SLAB, rows] = slab_pad

        for hd in range(N_HEADS):
            vt_ref[hd, p, 0:V_DIM, :] = vt[hd * V_DIM:(hd + 1) * V_DIM, :].astype(BF16)
            vt_ref[hd, p, V_DIM:V_ROWS, :] = ones_row
        pe_ss = jnp.sum(kpe * kpe, axis=-1, keepdims=True)
        pe_rot = _rope(kpe * kg, c, sa, sb)
        for hd in range(N_HEADS):
            sl = slice(hd * HEAD_SLAB, (hd + 1) * HEAD_SLAB)
            kh = kf[:, sl]
            rs = lax.rsqrt((jnp.sum(kh * kh, axis=-1, keepdims=True) + pe_ss) / QK_DIM + EPS)
            k_ref[rows, sl] = ((kh * kg + pe_rot) * rs).astype(BF16)

        u_ref[rows, :] = cv_a * jax.nn.sigmoid(cv_g)


def _proj(x2d, g, wqt, wkv, wpe, wcv, qlg, wuqt, kvlg, wk, wvt, qg, kg, tabs, tabs_t):
    n, d = x2d.shape
    tm = TM_PROJ
    s_blocks = SEQ // tm
    batch = n // SEQ
    hs = N_HEADS * HEAD_SLAB
    row = lambda w: pl.BlockSpec((tm, w), lambda i: (i, 0))
    tab = pl.BlockSpec((tm, HEAD_SLAB), lambda i: (i % s_blocks, 0))
    tab_t = pl.BlockSpec((QK_ROPE // 2, tm), lambda i: (0, i % s_blocks))
    qt_spec = pl.BlockSpec((None, hs, tm), lambda i: (i // s_blocks, 0, i % s_blocks))
    vt_spec = pl.BlockSpec((None, N_HEADS, PROJ_PARTS, V_ROWS, TK),
                           lambda i: (i // s_blocks, 0, i % s_blocks, 0, 0))
    consts = [g, wqt, wkv, wpe, wcv, qlg, wuqt, kvlg, wk, wvt, qg, kg]
    in_specs = [row(d)] + [_const_spec(a.shape) for a in consts] + [tab] * 3 + [tab_t] * 2
    return pl.pallas_call(
        _proj_kernel,
        grid=(n // tm,),
        in_specs=in_specs,
        out_specs=[qt_spec, row(hs), vt_spec, row(CONV_W)],
        out_shape=[jax.ShapeDtypeStruct((batch, hs, SEQ), BF16), jax.ShapeDtypeStruct((n, hs), BF16),
                   jax.ShapeDtypeStruct((batch, N_HEADS, SEQ // TK, V_ROWS, TK), BF16),
                   jax.ShapeDtypeStruct((n, CONV_W), F32)],
        compiler_params=_params(1),
        name="proj",
    )(x2d, *consts, *tabs, *tabs_t)


def _attn_kernel(qt_ref, qt_next_ref, k_ref, vt_ref, o_ref, sta_ref, stb_ref, m_ref, acc_ref):
    qi = pl.program_id(1)
    m_ref[...] = jnp.full(m_ref.shape, NEG, F32)
    acc_ref[...] = jnp.zeros(acc_ref.shape, F32)
    key_pos = lax.broadcasted_iota(jnp.int32, (TK, TQ), 0)
    qry_chunk = lax.broadcasted_iota(jnp.int32, (TK, TQ), 1) // CHUNK
    diag_masks = [((key_pos + d * TK) // CHUNK <= qry_chunk)[:, d * TK:] for d in range(DIAG_BLOCKS)]

    def scores(kj, hd, dst_ref, q_ref=qt_ref, c0=0):
        k0 = pl.multiple_of(kj * TK, TK)
        sl = slice(hd * HEAD_SLAB, (hd + 1) * HEAD_SLAB)
        dst_ref[hd, :, c0:] = _dot(k_ref[pl.ds(k0, TK), sl], q_ref[sl, c0:])

    def softmax_pv(kj, hd, src_ref, mask, c0):
        st = src_ref[hd, :, c0:]
        if mask is not None:
            st = jnp.where(mask, st, NEG)
        m_old = m_ref[hd, :, c0:]
        m_new = jnp.maximum(m_old, jnp.max(st, axis=0, keepdims=True))
        p = jnp.exp2(st - m_new).astype(BF16)
        acc_ref[hd, :, c0:] = jnp.exp2(m_old - m_new) * acc_ref[hd, :, c0:] + _dot(vt_ref[hd, kj], p)
        m_ref[hd, :, c0:] = m_new

    def stage(kj, cur_ref, nxt_ref, diag=None, next_diag=None, last=False):
        c0 = 0 if diag is None else diag * TK
        for hd in range(N_HEADS):
            if last:
                scores(0, hd, nxt_ref, qt_next_ref)
            else:
                scores(kj + 1, hd, nxt_ref, c0=0 if next_diag is None else next_diag * TK)
            softmax_pv(kj, hd, cur_ref, None if diag is None else diag_masks[diag], c0)

    @pl.when(qi == 0)
    def _():
        for hd in range(N_HEADS):
            scores(0, hd, sta_ref)

    bufs = (sta_ref, stb_ref)

    def body(t, carry):
        for s in range(ATTN_UNROLL):
            stage(ATTN_UNROLL * t + s, bufs[s % 2], bufs[(s + 1) % 2])
        return carry

    n_free = qi * DIAG_BLOCKS
    trips = n_free // ATTN_UNROLL
    lax.fori_loop(0, trips, body, 0)
    rest = n_free - trips * ATTN_UNROLL
    for r in range(0, ATTN_UNROLL, DIAG_BLOCKS):
        @pl.when(rest == r)
        def _(r=r):
            n_stages = r + DIAG_BLOCKS
            diag_of = lambda s: s - r if r <= s < n_stages else None
            for s in range(n_stages):
                stage(n_free - r + s, bufs[s % 2], bufs[(s + 1) % 2],
                      diag=diag_of(s), next_diag=diag_of(s + 1), last=s == n_stages - 1)

    for pair in range(N_HEADS // 2):
        halves = []
        for hd in (2 * pair, 2 * pair + 1):
            acc = acc_ref[hd]
            halves.append(acc[:V_DIM] / acc[V_DIM:V_DIM + 1])
        pair_t = jnp.concatenate(halves, axis=0)
        o_ref[:, pair * 2 * V_DIM:(pair + 1) * 2 * V_DIM] = pair_t.T.astype(BF16)


def _attn(qt, k, vt, batch):
    hs = N_HEADS * HEAD_SLAB
    k3 = k.reshape(batch, SEQ, hs)
    assert DIAG_BLOCKS % 2 == 0 and ATTN_UNROLL % DIAG_BLOCKS == 0
    last = SEQ // TQ - 1
    out = pl.pallas_call(
        _attn_kernel,
        grid=(batch, SEQ // TQ),
        in_specs=[pl.BlockSpec((None, hs, TQ), lambda b, qi: (b, 0, qi)),
                  pl.BlockSpec((None, hs, TQ), lambda b, qi: (b, 0, jnp.minimum(qi + 1, last))),
                  pl.BlockSpec((None, SEQ, hs), lambda b, qi: (b, 0, 0)),
                  pl.BlockSpec((None, N_HEADS, SEQ // TK, V_ROWS, TK), lambda b, qi: (b, 0, 0, 0, 0))],
        out_specs=pl.BlockSpec((None, TQ, ATTN_W), lambda b, qi: (b, qi, 0)),
        out_shape=jax.ShapeDtypeStruct((batch, SEQ, ATTN_W), BF16),
        scratch_shapes=[pltpu.VMEM((N_HEADS, TK, TQ), F32), pltpu.VMEM((N_HEADS, TK, TQ), F32),
                        pltpu.VMEM((N_HEADS, 1, TQ), F32),
                        pltpu.VMEM((N_HEADS, V_ROWS, TQ), F32)],
        compiler_params=_params(2),
        name="attn",
    )(qt, qt, k3, vt)
    return out.reshape(batch * SEQ, ATTN_W)


def _conv_module(halo, u, cw_ref, cb, lg, lb, ext_ref, dst_ref):
    rows = u.shape[0]
    ext_ref[0, 0:HALO, :] = halo
    ext_ref[0, HALO:HALO + rows, :] = u
    span = HALO + rows - SUBLANES
    for b in range(1, SUBLANES):
        ext_ref[b, 0:span, :] = ext_ref[0, b:b + span, :]

    base = HALO - (CONV_K - 1)
    groups = CONV_ROWS // SUBLANES
    for r in range(0, rows, CONV_ROWS):
        accs = [jnp.zeros((SUBLANES, CONV_W), F32) for _ in range(groups)]
        for j in range(CONV_K):
            b = (base + j) % SUBLANES
            a = r + base + j - b
            w = cw_ref[j]
            for g in range(groups):
                accs[g] = accs[g] + w * ext_ref[b, a + g * SUBLANES:a + (g + 1) * SUBLANES, :]
        y = jnp.concatenate(accs, axis=0) + cb
        mu = jnp.mean(y, axis=-1, keepdims=True)
        yc = y - mu
        z = yc * lax.rsqrt(jnp.mean(yc * yc, axis=-1, keepdims=True) + EPS) * lg + lb
        dst_ref[r:r + CONV_ROWS, :] = (z * jax.nn.sigmoid(z)).astype(BF16)


def _mix_ffn_kernel(halo_ref, u_ref, attn_ref, x_ref, cw_ref, cb_ref, lg_ref, lb_ref, woa_ref, woc_ref,
                    g_ref, wg_ref, wu_ref, wd_ref, pg_ref, o_ref, ext_ref, *conv_refs):
    first = (pl.program_id(0) % (SEQ // TM_MIX)) == 0
    cb, lg, lb = cb_ref[...], lg_ref[...], lb_ref[...]
    part = TM_MIX // MIX_PARTS
    for p, conv_ref in enumerate(conv_refs):
        rows = slice(p * part, (p + 1) * part)
        if p == 0:
            halo = jnp.where(first, 0.0, halo_ref[...])
        else:
            halo = u_ref[p * part - HALO:p * part, :]
        _conv_module(halo, u_ref[rows, :], cw_ref, cb, lg, lb, ext_ref, conv_ref)
        x = x_ref[rows, :] + _dot(attn_ref[rows, :], woa_ref[...]) + _dot(conv_ref[...], woc_ref[...])
        o_ref[rows, :] = _rms(_half_step_ffn(x, g_ref[...], wg_ref, wu_ref, wd_ref), pg_ref[...])


def _mix_ffn(u, attn, x2d, cw, cb, lg, lb, woa, woc, g, wg, wu, wd, pg):
    n, d = x2d.shape
    tm = TM_MIX
    part = tm // MIX_PARTS
    row = lambda w: pl.BlockSpec((tm, w), lambda i: (i, 0))
    halo = pl.BlockSpec((HALO, CONV_W), lambda i: (jnp.maximum(i * (tm // HALO) - 1, 0), 0))
    consts = [cw, cb, lg, lb, woa, woc, g.reshape(1, d), wg, wu, wd, pg.reshape(1, d)]
    return pl.pallas_call(
        _mix_ffn_kernel,
        grid=(n // tm,),
        in_specs=[halo, row(CONV_W), row(ATTN_W), row(d)] + [_const_spec(a.shape) for a in consts],
        out_specs=row(d),
        out_shape=jax.ShapeDtypeStruct((n, d), F32),
        scratch_shapes=[pltpu.VMEM((SUBLANES, HALO + part, CONV_W), F32)]
                       + [pltpu.VMEM((part, CONV_W), BF16)] * MIX_PARTS,
        compiler_params=_params(1),
        name="mix_ffn",
    )(u, u, attn, x2d, *consts)


def _rope_slab_tables(seq_len):
    pos = jnp.arange(seq_len, dtype=F32)
    inv_freq = 1.0 / (ROPE_THETA ** (jnp.arange(0, QK_ROPE, 2, dtype=F32) / QK_ROPE))
    ang = pos[:, None] * inv_freq[None, :]
    cos, sin = jnp.cos(ang), jnp.sin(ang)
    half = QK_ROPE // 2
    ones = jnp.ones((seq_len, QK_NOPE), F32)
    z = lambda w: jnp.zeros((seq_len, w), F32)
    tail = HEAD_SLAB - QK_DIM
    c_tab = jnp.concatenate([ones, cos, cos, z(tail)], axis=1)
    sa_tab = jnp.concatenate([z(QK_NOPE), -sin, z(half), z(tail)], axis=1)
    sb_tab = jnp.concatenate([z(QK_NOPE), z(half), sin, z(tail)], axis=1)
    return (c_tab, sa_tab, sb_tab), (cos.T, sin.T)


def _head_slabs(w, width, offset=0):
    kdim = w.shape[0]
    w3 = w.reshape(kdim, N_HEADS, width)
    w3 = jnp.pad(w3, ((0, 0), (0, 0), (offset, HEAD_SLAB - width - offset)))
    return w3.reshape(kdim, N_HEADS * HEAD_SLAB)


def _slab_vec(g, offset=0):
    return jnp.pad(g, (offset, HEAD_SLAB - g.shape[0] - offset)).reshape(1, HEAD_SLAB)


def kernel(x, ffn1_norm, ffn1_w_gate, ffn1_w_up, ffn1_w_down, mix_norm, w_in, q_latent_norm, w_uq,
           kv_latent_norm, w_ukv, q_norm, k_norm, conv_w, conv_b, conv_ln_g, conv_ln_b, w_out,
           ffn2_norm, ffn2_w_gate, ffn2_w_up, ffn2_w_down, post_norm):
    batch, seq, d = x.shape
    assert (seq, d) == (SEQ, D_MODEL)
    depth = ffn1_norm.shape[0]
    tabs, tabs_t = _rope_slab_tables(seq)
    o_kv, o_pe, o_cv = Q_LORA, Q_LORA + KV_LORA, Q_LORA + KV_LORA + QK_ROPE
    bf = lambda a: a.astype(BF16)
    row = lambda a: a.reshape(1, -1)
    lanes = lambda a: jnp.broadcast_to(a[:, None], (a.shape[0], TK))

    x2d = x.reshape(batch * seq, d)
    for l in range(depth):
        x2d = _ffn(x2d, ffn1_norm[l], *_ffn_weights_bf16(l, ffn1_w_gate, ffn1_w_up, ffn1_w_down))

        wi = w_in[l]
        wpe = jnp.pad(wi[:, o_pe:o_cv], ((0, 0), (QK_NOPE, HEAD_SLAB - QK_DIM)))
        ukv = w_ukv[l].reshape(KV_LORA, N_HEADS, QK_NOPE + V_DIM)
        wk = _head_slabs(ukv[:, :, :QK_NOPE].reshape(KV_LORA, N_HEADS * QK_NOPE), QK_NOPE)
        wvt = ukv[:, :, QK_NOPE:].reshape(KV_LORA, ATTN_W).T
        qg = lanes(q_norm[l] * (QK_DIM ** -0.5 * LOG2E))
        qt, k, vt, u = _proj(
            x2d, row(mix_norm[l]), bf(wi[:, :o_kv].T), bf(wi[:, o_kv:o_pe]), bf(wpe), bf(wi[:, o_cv:]),
            lanes(q_latent_norm[l]), bf(_head_slabs(w_uq[l], QK_DIM).T), row(kv_latent_norm[l]), bf(wk), bf(wvt),
            qg, _slab_vec(k_norm[l]), tabs, tabs_t)

        attn = _attn(qt, k, vt, batch)

        wo = w_out[l]
        cw = jnp.broadcast_to(conv_w[l][:, None, :], (CONV_K, SUBLANES, CONV_W))
        x2d = _mix_ffn(u, attn, x2d, cw, row(conv_b[l]), row(conv_ln_g[l]), row(conv_ln_b[l]),
                       bf(wo[:ATTN_W]), bf(wo[ATTN_W:]),
                       ffn2_norm[l], *_ffn_weights_bf16(l, ffn2_w_gate, ffn2_w_up, ffn2_w_down), post_norm[l])
    return x2d.reshape(batch, seq, d)
```

```python
import jax
import jax.numpy as jnp
from jax import lax
from jax.experimental import pallas as pl
from jax.experimental.pallas import tpu as pltpu

D_MODEL = 1024
SEQ = 4096
CHUNK = 64
N_HEADS = 8
QK_NOPE = 64
QK_ROPE = 32
QK_DIM = QK_NOPE + QK_ROPE
V_DIM = 64
Q_LORA = 384
KV_LORA = 256
ATTN_W = N_HEADS * V_DIM
CONV_W = D_MODEL - ATTN_W
CONV_K = 31
D_FF = 2816
ROPE_THETA = 10000.0
EPS = 1e-6
NEG = -1e30

LANES = 128
SUBLANES = 8
HEAD_SLAB = LANES
HALO = 32
VMEM_LIMIT = 56 * 1024 * 1024

TM_FFN = 1024
TM_PROJ = 1024
TM_MIX = 512
TQ = 512
TK = 256
PROJ_PARTS = TM_PROJ // TK
MIX_PARTS = 2
DIAG_BLOCKS = TQ // TK
ATTN_UNROLL = 4
CONV_ROWS = 32
CAST_STEPS = 4
V_ROWS = 80
LOG2E = 1.4426950408889634
F_CHUNKS = ((0, 1024), (1024, 2048), (2048, D_FF))

BF16 = jnp.bfloat16
F32 = jnp.float32


def _rms(x, g):
    y = x * lax.rsqrt(jnp.mean(x * x, axis=-1, keepdims=True) + EPS)
    return y * g


def _dot(a, b):
    return jnp.dot(a, b, preferred_element_type=F32)


def _const_spec(shape):
    return pl.BlockSpec(shape, lambda *_: (0,) * len(shape), pipeline_mode=pl.Buffered(1))


def _params(n_axes):
    return pltpu.CompilerParams(dimension_semantics=("arbitrary",) * n_axes, vmem_limit_bytes=VMEM_LIMIT)


def _half_step_ffn(x, g, wg_ref, wu_ref, wd_ref):
    h = _rms(x, g).astype(BF16)
    y = jnp.zeros_like(x)
    for lo, hi in F_CHUNKS:
        gate = _dot(h, wg_ref[:, lo:hi])
        up = _dot(h, wu_ref[:, lo:hi])
        a = (gate * jax.nn.sigmoid(gate) * up).astype(BF16)
        y = y + _dot(a, wd_ref[lo:hi, :])
    return x + 0.5 * y


def _ffn_kernel(x_ref, g_ref, wg_ref, wu_ref, wd_ref, o_ref):
    o_ref[...] = _half_step_ffn(x_ref[...], g_ref[...], wg_ref, wu_ref, wd_ref)


def _ffn(x2d, g, wg, wu, wd):
    n, d = x2d.shape
    row = pl.BlockSpec((TM_FFN, d), lambda i: (i, 0))
    return pl.pallas_call(
        _ffn_kernel,
        grid=(n // TM_FFN,),
        in_specs=[row, _const_spec((1, d)), _const_spec(wg.shape), _const_spec(wu.shape), _const_spec(wd.shape)],
        out_specs=row,
        out_shape=jax.ShapeDtypeStruct((n, d), F32),
        compiler_params=_params(1),
        name="ffn",
    )(x2d, g.reshape(1, d), wg, wu, wd)


def _cast_kernel(*refs):
    n = len(refs) // 2
    for src, dst in zip(refs[:n], refs[n:]):
        dst[...] = src[...].astype(BF16)


def _ffn_weights_bf16(layer, *stacked):
    specs, out_specs, out_shapes = [], [], []
    for w in stacked:
        _, rows, cols = w.shape
        blk = rows // CAST_STEPS
        specs.append(pl.BlockSpec((None, blk, cols), lambda i, layer=layer: (layer, i, 0)))
        out_specs.append(pl.BlockSpec((blk, cols), lambda i: (i, 0)))
        out_shapes.append(jax.ShapeDtypeStruct((rows, cols), BF16))
    return pl.pallas_call(
        _cast_kernel,
        grid=(CAST_STEPS,),
        in_specs=specs,
        out_specs=out_specs,
        out_shape=out_shapes,
        compiler_params=_params(1),
        name="cast_w",
    )(*stacked)


def _rope(x, c, sa, sb):
    return x * c + pltpu.roll(x, HEAD_SLAB - QK_ROPE // 2, 1) * sa + pltpu.roll(x, QK_ROPE // 2, 1) * sb


def _proj_kernel(x_ref, g_ref, wqt_ref, wkv_ref, wpe_ref, wcv_ref, qlg_ref, wuqt_ref, kvlg_ref, wk_ref, wvt_ref,
                 qg_ref, kg_ref, c_ref, sa_ref, sb_ref, cos_ref, sin_ref, qt_ref, k_ref, vt_ref, u_ref):
    nt = (((1,), (1,)), ((), ()))
    half = QK_ROPE // 2
    r1, r2 = QK_NOPE, QK_NOPE + half
    qg, kg = qg_ref[...], kg_ref[...]
    slab_pad = jnp.zeros((HEAD_SLAB - QK_DIM, TK), BF16)
    ones_row = jnp.where(lax.broadcasted_iota(jnp.int32, (V_ROWS - V_DIM, TK), 0) == 0, 1.0, 0.0).astype(BF16)

    for p in range(PROJ_PARTS):
        rows = slice(p * TK, (p + 1) * TK)
        h = _rms(x_ref[rows, :], g_ref[...]).astype(BF16)
        c, sa, sb = c_ref[rows, :], sa_ref[rows, :], sb_ref[rows, :]
        cos_t, sin_t = cos_ref[:, rows], sin_ref[:, rows]

        cqt_raw = lax.dot_general(wqt_ref[...], h, nt, preferred_element_type=F32)
        ckv_raw = _dot(h, wkv_ref[...])
        kpe = _dot(h, wpe_ref[...])
        cv_a = _dot(h, wcv_ref[:, :CONV_W])
        cqt = cqt_raw * lax.rsqrt(jnp.mean(cqt_raw * cqt_raw, axis=0, keepdims=True) + EPS) * qlg_ref[...]
        ckv = _rms(ckv_raw, kvlg_ref[...]).astype(BF16)
        qft = _dot(wuqt_ref[...], cqt.astype(BF16))
        kf = _dot(ckv, wk_ref[...])
        vt = lax.dot_general(wvt_ref[...], ckv, nt, preferred_element_type=F32)
        cv_g = _dot(h, wcv_ref[:, CONV_W:])

        for hd in range(N_HEADS):
            base = hd * HEAD_SLAB
            qh = qft[base:base + QK_DIM, :]
            qn = qh * lax.rsqrt(jnp.sum(qh * qh, axis=0, keepdims=True) / QK_DIM + EPS) * qg
            x1, x2 = qn[r1:r2], qn[r2:QK_DIM]
            qt_ref[base:base + r1, rows] = qn[:r1].astype(BF16)
            qt_ref[base + r1:base + r2, rows] = (x1 * cos_t - x2 * sin_t).astype(BF16)
            qt_ref[base + r2:base + QK_DIM, rows] = (x1 * sin_t + x2 * cos_t).astype(BF16)
            qt_ref[base + QK_DIM:base + HEAD_SLAB, rows] = slab_pad

        for hd in range(N_HEADS):
            vt_ref[hd, p, 0:V_DIM, :] = vt[hd * V_DIM:(hd + 1) * V_DIM, :].astype(BF16)
            vt_ref[hd, p, V_DIM:V_ROWS, :] = ones_row
        pe_ss = jnp.sum(kpe * kpe, axis=-1, keepdims=True)
        pe_rot = _rope(kpe * kg, c, sa, sb)
        for hd in range(N_HEADS):
            sl = slice(hd * HEAD_SLAB, (hd + 1) * HEAD_SLAB)
            kh = kf[:, sl]
            rs = lax.rsqrt((jnp.sum(kh * kh, axis=-1, keepdims=True) + pe_ss) / QK_DIM + EPS)
            k_ref[rows, sl] = ((kh * kg + pe_rot) * rs).astype(BF16)

        u_ref[rows, :] = cv_a * jax.nn.sigmoid(cv_g)


def _proj(x2d, g, wqt, wkv, wpe, wcv, qlg, wuqt, kvlg, wk, wvt, qg, kg, tabs, tabs_t):
    n, d = x2d.shape
    tm = TM_PROJ
    s_blocks = SEQ // tm
    batch = n // SEQ
    hs = N_HEADS * HEAD_SLAB
    row = lambda w: pl.BlockSpec((tm, w), lambda i: (i, 0))
    tab = pl.BlockSpec((tm, HEAD_SLAB), lambda i: (i % s_blocks, 0))
    tab_t = pl.BlockSpec((QK_ROPE // 2, tm), lambda i: (0, i % s_blocks))
    qt_spec = pl.BlockSpec((None, hs, tm), lambda i: (i // s_blocks, 0, i % s_blocks))
    vt_spec = pl.BlockSpec((None, N_HEADS, PROJ_PARTS, V_ROWS, TK),
                           lambda i: (i // s_blocks, 0, i % s_blocks, 0, 0))
    consts = [g, wqt, wkv, wpe, wcv, qlg, wuqt, kvlg, wk, wvt, qg, kg]
    in_specs = [row(d)] + [_const_spec(a.shape) for a in consts] + [tab] * 3 + [tab_t] * 2
    return pl.pallas_call(
        _proj_kernel,
        grid=(n // tm,),
        in_specs=in_specs,
        out_specs=[qt_spec, row(hs), vt_spec, row(CONV_W)],
        out_shape=[jax.ShapeDtypeStruct((batch, hs, SEQ), BF16), jax.ShapeDtypeStruct((n, hs), BF16),
                   jax.ShapeDtypeStruct((batch, N_HEADS, SEQ // TK, V_ROWS, TK), BF16),
                   jax.ShapeDtypeStruct((n, CONV_W), F32)],
        compiler_params=_params(1),
        name="proj",
    )(x2d, *consts, *tabs, *tabs_t)


def _attn_kernel(qt_ref, qt_next_ref, k_ref, vt_ref, o_ref, sta_ref, stb_ref, m_ref, acc_ref):
    qi = pl.program_id(1)
    m_ref[...] = jnp.full(m_ref.shape, NEG, F32)
    acc_ref[...] = jnp.zeros(acc_ref.shape, F32)
    key_pos = lax.broadcasted_iota(jnp.int32, (TK, TQ), 0)
    qry_chunk = lax.broadcasted_iota(jnp.int32, (TK, TQ), 1) // CHUNK
    diag_masks = [((key_pos + d * TK) // CHUNK <= qry_chunk)[:, d * TK:] for d in range(DIAG_BLOCKS)]

    def scores(kj, hd, dst_ref, q_ref=qt_ref, c0=0):
        k0 = pl.multiple_of(kj * TK, TK)
        sl = slice(hd * HEAD_SLAB, (hd + 1) * HEAD_SLAB)
        dst_ref[hd, :, c0:] = _dot(k_ref[pl.ds(k0, TK), sl], q_ref[sl, c0:])

    def softmax_pv(kj, hd, src_ref, mask, c0):
        st = src_ref[hd, :, c0:]
        if mask is not None:
            st = jnp.where(mask, st, NEG)
        m_old = m_ref[hd, :, c0:]
        m_new = jnp.maximum(m_old, jnp.max(st, axis=0, keepdims=True))
        p = jnp.exp2(st - m_new).astype(BF16)
        acc_ref[hd, :, c0:] = jnp.exp2(m_old - m_new) * acc_ref[hd, :, c0:] + _dot(vt_ref[hd, kj], p)
        m_ref[hd, :, c0:] = m_new

    def stage(kj, cur_ref, nxt_ref, diag=None, next_diag=None, last=False):
        c0 = 0 if diag is None else diag * TK
        for hd in range(N_HEADS):
            if last:
                scores(0, hd, nxt_ref, qt_next_ref)
            else:
                scores(kj + 1, hd, nxt_ref, c0=0 if next_diag is None else next_diag * TK)
            softmax_pv(kj, hd, cur_ref, None if diag is None else diag_masks[diag], c0)

    @pl.when(qi == 0)
    def _():
        for hd in range(N_HEADS):
            scores(0, hd, sta_ref)

    bufs = (sta_ref, stb_ref)

    def body(t, carry):
        for s in range(ATTN_UNROLL):
            stage(ATTN_UNROLL * t + s, bufs[s % 2], bufs[(s + 1) % 2])
        return carry

    n_free = qi * DIAG_BLOCKS
    trips = n_free // ATTN_UNROLL
    lax.fori_loop(0, trips, body, 0)
    rest = n_free - trips * ATTN_UNROLL
    for r in range(0, ATTN_UNROLL, DIAG_BLOCKS):
        @pl.when(rest == r)
        def _(r=r):
            n_stages = r + DIAG_BLOCKS
            diag_of = lambda s: s - r if r <= s < n_stages else None
            for s in range(n_stages):
                stage(n_free - r + s, bufs[s % 2], bufs[(s + 1) % 2],
                      diag=diag_of(s), next_diag=diag_of(s + 1), last=s == n_stages - 1)

    for pair in range(N_HEADS // 2):
        halves = []
        for hd in (2 * pair, 2 * pair + 1):
            acc = acc_ref[hd]
            halves.append(acc[:V_DIM] / acc[V_DIM:V_DIM + 1])
        pair_t = jnp.concatenate(halves, axis=0)
        o_ref[:, pair * 2 * V_DIM:(pair + 1) * 2 * V_DIM] = pair_t.T.astype(BF16)


def _attn(qt, k, vt, batch):
    hs = N_HEADS * HEAD_SLAB
    k3 = k.reshape(batch, SEQ, hs)
    assert DIAG_BLOCKS % 2 == 0 and ATTN_UNROLL % DIAG_BLOCKS == 0
    last = SEQ // TQ - 1
    out = pl.pallas_call(
        _attn_kernel,
        grid=(batch, SEQ // TQ),
        in_specs=[pl.BlockSpec((None, hs, TQ), lambda b, qi: (b, 0, qi)),
                  pl.BlockSpec((None, hs, TQ), lambda b, qi: (b, 0, jnp.minimum(qi + 1, last))),
                  pl.BlockSpec((None, SEQ, hs), lambda b, qi: (b, 0, 0)),
                  pl.BlockSpec((None, N_HEADS, SEQ // TK, V_ROWS, TK), lambda b, qi: (b, 0, 0, 0, 0))],
        out_specs=pl.BlockSpec((None, TQ, ATTN_W), lambda b, qi: (b, qi, 0)),
        out_shape=jax.ShapeDtypeStruct((batch, SEQ, ATTN_W), BF16),
        scratch_shapes=[pltpu.VMEM((N_HEADS, TK, TQ), F32), pltpu.VMEM((N_HEADS, TK, TQ), F32),
                        pltpu.VMEM((N_HEADS, 1, TQ), F32),
                        pltpu.VMEM((N_HEADS, V_ROWS, TQ), F32)],
        compiler_params=_params(2),
        name="attn",
    )(qt, qt, k3, vt)
    return out.reshape(batch * SEQ, ATTN_W)


def _conv_module(halo, u, cw_ref, cb, lg, lb, ext_ref, dst_ref):
    rows = u.shape[0]
    ext_ref[0, 0:HALO, :] = halo
    ext_ref[0, HALO:HALO + rows, :] = u
    span = HALO + rows - SUBLANES
    for b in range(1, SUBLANES):
        ext_ref[b, 0:span, :] = ext_ref[0, b:b + span, :]

    base = HALO - (CONV_K - 1)
    groups = CONV_ROWS // SUBLANES
    for r in range(0, rows, CONV_ROWS):
        accs = [jnp.zeros((SUBLANES, CONV_W), F32) for _ in range(groups)]
        for j in range(CONV_K):
            b = (base + j) % SUBLANES
            a = r + base + j - b
            w = cw_ref[j]
            for g in range(groups):
                accs[g] = accs[g] + w * ext_ref[b, a + g * SUBLANES:a + (g + 1) * SUBLANES, :]
        y = jnp.concatenate(accs, axis=0) + cb
        mu = jnp.mean(y, axis=-1, keepdims=True)
        yc = y - mu
        z = yc * lax.rsqrt(jnp.mean(yc * yc, axis=-1, keepdims=True) + EPS) * lg + lb
        dst_ref[r:r + CONV_ROWS, :] = (z * jax.nn.sigmoid(z)).astype(BF16)


def _mix_ffn_kernel(halo_ref, u_ref, attn_ref, x_ref, cw_ref, cb_ref, lg_ref, lb_ref, woa_ref, woc_ref,
                    g_ref, wg_ref, wu_ref, wd_ref, pg_ref, o_ref, ext_ref, *conv_refs):
    first = (pl.program_id(0) % (SEQ // TM_MIX)) == 0
    cb, lg, lb = cb_ref[...], lg_ref[...], lb_ref[...]
    part = TM_MIX // MIX_PARTS
    for p, conv_ref in enumerate(conv_refs):
        rows = slice(p * part, (p + 1) * part)
        if p == 0:
            halo = jnp.where(first, 0.0, halo_ref[...])
        else:
            halo = u_ref[p * part - HALO:p * part, :]
        _conv_module(halo, u_ref[rows, :], cw_ref, cb, lg, lb, ext_ref, conv_ref)
        x = x_ref[rows, :] + _dot(attn_ref[rows, :], woa_ref[...]) + _dot(conv_ref[...], woc_ref[...])
        o_ref[rows, :] = _rms(_half_step_ffn(x, g_ref[...], wg_ref, wu_ref, wd_ref), pg_ref[...])


def _mix_ffn(u, attn, x2d, cw, cb, lg, lb, woa, woc, g, wg, wu, wd, pg):
    n, d = x2d.shape
    tm = TM_MIX
    part = tm // MIX_PARTS
    row = lambda w: pl.BlockSpec((tm, w), lambda i: (i, 0))
    halo = pl.BlockSpec((HALO, CONV_W), lambda i: (jnp.maximum(i * (tm // HALO) - 1, 0), 0))
    consts = [cw, cb, lg, lb, woa, woc, g.reshape(1, d), wg, wu, wd, pg.reshape(1, d)]
    return pl.pallas_call(
        _mix_ffn_kernel,
        grid=(n // tm,),
        in_specs=[halo, row(CONV_W), row(ATTN_W), row(d)] + [_const_spec(a.shape) for a in consts],
        out_specs=row(d),
        out_shape=jax.ShapeDtypeStruct((n, d), F32),
        scratch_shapes=[pltpu.VMEM((SUBLANES, HALO + part, CONV_W), F32)]
                       + [pltpu.VMEM((part, CONV_W), BF16)] * MIX_PARTS,
        compiler_params=_params(1),
        name="mix_ffn",
    )(u, u, attn, x2d, *consts)


def _rope_slab_tables(seq_len):
    pos = jnp.arange(seq_len, dtype=F32)
    inv_freq = 1.0 / (ROPE_THETA ** (jnp.arange(0, QK_ROPE, 2, dtype=F32) / QK_ROPE))
    ang = pos[:, None] * inv_freq[None, :]
    cos, sin = jnp.cos(ang), jnp.sin(ang)
    half = QK_ROPE // 2
    ones = jnp.ones((seq_len, QK_NOPE), F32)
    z = lambda w: jnp.zeros((seq_len, w), F32)
    tail = HEAD_SLAB - QK_DIM
    c_tab = jnp.concatenate([ones, cos, cos, z(tail)], axis=1)
    sa_tab = jnp.concatenate([z(QK_NOPE), -sin, z(half), z(tail)], axis=1)
    sb_tab = jnp.concatenate([z(QK_NOPE), z(half), sin, z(tail)], axis=1)
    return (c_tab, sa_tab, sb_tab), (cos.T, sin.T)


def _head_slabs(w, width, offset=0):
    kdim = w.shape[0]
    w3 = w.reshape(kdim, N_HEADS, width)
    w3 = jnp.pad(w3, ((0, 0), (0, 0), (offset, HEAD_SLAB - width - offset)))
    return w3.reshape(kdim, N_HEADS * HEAD_SLAB)


def _slab_vec(g, offset=0):
    return jnp.pad(g, (offset, HEAD_SLAB - g.shape[0] - offset)).reshape(1, HEAD_SLAB)


def kernel(x, ffn1_norm, ffn1_w_gate, ffn1_w_up, ffn1_w_down, mix_norm, w_in, q_latent_norm, w_uq,
           kv_latent_norm, w_ukv, q_norm, k_norm, conv_w, conv_b, conv_ln_g, conv_ln_b, w_out,
           ffn2_norm, ffn2_w_gate, ffn2_w_up, ffn2_w_down, post_norm):
    batch, seq, d = x.shape
    assert (seq, d) == (SEQ, D_MODEL)
    depth = ffn1_norm.shape[0]
    tabs, tabs_t = _rope_slab_tables(seq)
    o_kv, o_pe, o_cv = Q_LORA, Q_LORA + KV_LORA, Q_LORA + KV_LORA + QK_ROPE
    bf = lambda a: a.astype(BF16)
    row = lambda a: a.reshape(1, -1)
    lanes = lambda a: jnp.broadcast_to(a[:, None], (a.shape[0], TK))

    x2d = x.reshape(batch * seq, d)
    for l in range(depth):
        x2d = _ffn(x2d, ffn1_norm[l], *_ffn_weights_bf16(l, ffn1_w_gate, ffn1_w_up, ffn1_w_down))

        wi = w_in[l]
        wpe = jnp.pad(wi[:, o_pe:o_cv], ((0, 0), (QK_NOPE, HEAD_SLAB - QK_DIM)))
        ukv = w_ukv[l].reshape(KV_LORA, N_HEADS, QK_NOPE + V_DIM)
        wk = _head_slabs(ukv[:, :, :QK_NOPE].reshape(KV_LORA, N_HEADS * QK_NOPE), QK_NOPE)
        wvt = ukv[:, :, QK_NOPE:].reshape(KV_LORA, ATTN_W).T
        qg = lanes(q_norm[l] * (QK_DIM ** -0.5 * LOG2E))
        qt, k, vt, u = _proj(
            x2d, row(mix_norm[l]), bf(wi[:, :o_kv].T), bf(wi[:, o_kv:o_pe]), bf(wpe), bf(wi[:, o_cv:]),
            lanes(q_latent_norm[l]), bf(_head_slabs(w_uq[l], QK_DIM).T), row(kv_latent_norm[l]), bf(wk), bf(wvt),
            qg, _slab_vec(k_norm[l]), tabs, tabs_t)

        attn = _attn(qt, k, vt, batch)

        wo = w_out[l]
        cw = jnp.broadcast_to(conv_w[l][:, None, :], (CONV_K, SUBLANES, CONV_W))
        x2d = _mix_ffn(u, attn, x2d, cw, row(conv_b[l]), row(conv_ln_g[l]), row(conv_ln_b[l]),
                       bf(wo[:ATTN_W]), bf(wo[ATTN_W:]),
                       ffn2_norm[l], *_ffn_weights_bf16(l, ffn2_w_gate, ffn2_w_up, ffn2_w_down), post_norm[l])
    return x2d.reshape(batch, seq, d)
```

```python
import jax
import jax.numpy as jnp
from jax import lax
from jax.experimental import pallas as pl
from jax.experimental.pallas import tpu as pltpu

D_MODEL = 1024
SEQ = 4096
CHUNK = 64
N_HEADS = 8
QK_NOPE = 64
QK_ROPE = 32
QK_DIM = QK_NOPE + QK_ROPE
V_DIM = 64
Q_LORA = 384
KV_LORA = 256
ATTN_W = N_HEADS * V_DIM
CONV_W = D_MODEL - ATTN_W
CONV_K = 31
D_FF = 2816
ROPE_THETA = 10000.0
EPS = 1e-6
NEG = -1e30

LANES = 128
SUBLANES = 8
HEAD_SLAB = LANES
HALO = 32
VMEM_LIMIT = 56 * 1024 * 1024

TM_FFN = 1024
TM_PROJ = 1024
TM_MIX = 512
TQ = 512
TK = 256
PROJ_PARTS = TM_PROJ // TK
MIX_PARTS = 1
DIAG_BLOCKS = TQ // TK
ATTN_UNROLL = 4
CONV_ROWS = 32
CAST_STEPS = 4
V_ROWS = 80
LOG2E = 1.4426950408889634
F_CHUNKS = ((0, 1024), (1024, 2048), (2048, D_FF))

BF16 = jnp.bfloat16
F32 = jnp.float32


def _rms(x, g):
    y = x * lax.rsqrt(jnp.mean(x * x, axis=-1, keepdims=True) + EPS)
    return y * g


def _dot(a, b):
    return jnp.dot(a, b, preferred_element_type=F32)


def _const_spec(shape):
    return pl.BlockSpec(shape, lambda *_: (0,) * len(shape), pipeline_mode=pl.Buffered(1))


def _params(n_axes):
    return pltpu.CompilerParams(dimension_semantics=("arbitrary",) * n_axes, vmem_limit_bytes=VMEM_LIMIT)


def _half_step_ffn(x, g, wg_ref, wu_ref, wd_ref):
    h = _rms(x, g).astype(BF16)
    y = jnp.zeros_like(x)
    for lo, hi in F_CHUNKS:
        gate = _dot(h, wg_ref[:, lo:hi])
        up = _dot(h, wu_ref[:, lo:hi])
        a = (gate * jax.nn.sigmoid(gate) * up).astype(BF16)
        y = y + _dot(a, wd_ref[lo:hi, :])
    return x + 0.5 * y


def _ffn_kernel(x_ref, g_ref, wg_ref, wu_ref, wd_ref, o_ref):
    o_ref[...] = _half_step_ffn(x_ref[...], g_ref[...], wg_ref, wu_ref, wd_ref)


def _ffn(x2d, g, wg, wu, wd):
    n, d = x2d.shape
    row = pl.BlockSpec((TM_FFN, d), lambda i: (i, 0))
    return pl.pallas_call(
        _ffn_kernel,
        grid=(n // TM_FFN,),
        in_specs=[row, _const_spec((1, d)), _const_spec(wg.shape), _const_spec(wu.shape), _const_spec(wd.shape)],
        out_specs=row,
        out_shape=jax.ShapeDtypeStruct((n, d), F32),
        compiler_params=_params(1),
        name="ffn",
    )(x2d, g.reshape(1, d), wg, wu, wd)


def _cast_kernel(*refs):
    n = len(refs) // 2
    for src, dst in zip(refs[:n], refs[n:]):
        dst[...] = src[...].astype(BF16)


def _ffn_weights_bf16(layer, *stacked):
    specs, out_specs, out_shapes = [], [], []
    for w in stacked:
        _, rows, cols = w.shape
        blk = rows // CAST_STEPS
        specs.append(pl.BlockSpec((None, blk, cols), lambda i, layer=layer: (layer, i, 0)))
        out_specs.append(pl.BlockSpec((blk, cols), lambda i: (i, 0)))
        out_shapes.append(jax.ShapeDtypeStruct((rows, cols), BF16))
    return pl.pallas_call(
        _cast_kernel,
        grid=(CAST_STEPS,),
        in_specs=specs,
        out_specs=out_specs,
        out_shape=out_shapes,
        compiler_params=_params(1),
        name="cast_w",
    )(*stacked)


def _rope(x, c, sa, sb):
    return x * c + pltpu.roll(x, HEAD_SLAB - QK_ROPE // 2, 1) * sa + pltpu.roll(x, QK_ROPE // 2, 1) * sb


def _proj_kernel(x_ref, g_ref, wqt_ref, wkv_ref, wpe_ref, wcv_ref, qlg_ref, wuqt_ref, kvlg_ref, wk_ref, wvt_ref,
                 qg_ref, kg_ref, c_ref, sa_ref, sb_ref, cos_ref, sin_ref, qt_ref, k_ref, vt_ref, u_ref):
    nt = (((1,), (1,)), ((), ()))
    half = QK_ROPE // 2
    r1, r2 = QK_NOPE, QK_NOPE + half
    qg, kg = qg_ref[...], kg_ref[...]
    slab_pad = jnp.zeros((HEAD_SLAB - QK_DIM, TK), BF16)
    ones_row = jnp.where(lax.broadcasted_iota(jnp.int32, (V_ROWS - V_DIM, TK), 0) == 0, 1.0, 0.0).astype(BF16)

    for p in range(PROJ_PARTS):
        rows = slice(p * TK, (p + 1) * TK)
        h = _rms(x_ref[rows, :], g_ref[...]).astype(BF16)
        c, sa, sb = c_ref[rows, :], sa_ref[rows, :], sb_ref[rows, :]
        cos_t, sin_t = cos_ref[:, rows], sin_ref[:, rows]

        cqt_raw = lax.dot_general(wqt_ref[...], h, nt, preferred_element_type=F32)
        ckv_raw = _dot(h, wkv_ref[...])
        kpe = _dot(h, wpe_ref[...])
        cv_a = _dot(h, wcv_ref[:, :CONV_W])
        cqt = cqt_raw * lax.rsqrt(jnp.mean(cqt_raw * cqt_raw, axis=0, keepdims=True) + EPS) * qlg_ref[...]
        ckv = _rms(ckv_raw, kvlg_ref[...]).astype(BF16)
        qft = _dot(wuqt_ref[...], cqt.astype(BF16))
        kf = _dot(ckv, wk_ref[...])
        vt = lax.dot_general(wvt_ref[...], ckv, nt, preferred_element_type=F32)
        cv_g = _dot(h, wcv_ref[:, CONV_W:])

        for hd in range(N_HEADS):
            base = hd * HEAD_SLAB
            qh = qft[base:base + QK_DIM, :]
            qn = qh * lax.rsqrt(jnp.sum(qh * qh, axis=0, keepdims=True) / QK_DIM + EPS) * qg
            x1, x2 = qn[r1:r2], qn[r2:QK_DIM]
            qt_ref[base:base + r1, rows] = qn[:r1].astype(BF16)
            qt_ref[base + r1:base + r2, rows] = (x1 * cos_t - x2 * sin_t).astype(BF16)
            qt_ref[base + r2:base + QK_DIM, rows] = (x1 * sin_t + x2 * cos_t).astype(BF16)
            qt_ref[base + QK_DIM:base + HEAD_SLAB, rows] = slab_pad

        for hd in range(N_HEADS):
            vt_ref[hd, p, 0:V_DIM, :] = vt[hd * V_DIM:(hd + 1) * V_DIM, :].astype(BF16)
            vt_ref[hd, p, V_DIM:V_ROWS, :] = ones_row
        pe_ss = jnp.sum(kpe * kpe, axis=-1, keepdims=True)
        pe_rot = _rope(kpe * kg, c, sa, sb)
        for hd in range(N_HEADS):
            sl = slice(hd * HEAD_SLAB, (hd + 1) * HEAD_SLAB)
            kh = kf[:, sl]
            rs = lax.rsqrt((jnp.sum(kh * kh, axis=-1, keepdims=True) + pe_ss) / QK_DIM + EPS)
            k_ref[rows, sl] = ((kh * kg + pe_rot) * rs).astype(BF16)

        u_ref[rows, :] = cv_a * jax.nn.sigmoid(cv_g)


def _proj(x2d, g, wqt, wkv, wpe, wcv, qlg, wuqt, kvlg, wk, wvt, qg, kg, tabs, tabs_t):
    n, d = x2d.shape
    tm = TM_PROJ
    s_blocks = SEQ // tm
    batch = n // SEQ
    hs = N_HEADS * HEAD_SLAB
    row = lambda w: pl.BlockSpec((tm, w), lambda i: (i, 0))
    tab = pl.BlockSpec((tm, HEAD_SLAB), lambda i: (i % s_blocks, 0))
    tab_t = pl.BlockSpec((QK_ROPE // 2, tm), lambda i: (0, i % s_blocks))
    qt_spec = pl.BlockSpec((None, hs, tm), lambda i: (i // s_blocks, 0, i % s_blocks))
    vt_spec = pl.BlockSpec((None, N_HEADS, PROJ_PARTS, V_ROWS, TK),
                           lambda i: (i // s_blocks, 0, i % s_blocks, 0, 0))
    consts = [g, wqt, wkv, wpe, wcv, qlg, wuqt, kvlg, wk, wvt, qg, kg]
    in_specs = [row(d)] + [_const_spec(a.shape) for a in consts] + [tab] * 3 + [tab_t] * 2
    return pl.pallas_call(
        _proj_kernel,
        grid=(n // tm,),
        in_specs=in_specs,
        out_specs=[qt_spec, row(hs), vt_spec, row(CONV_W)],
        out_shape=[jax.ShapeDtypeStruct((batch, hs, SEQ), BF16), jax.ShapeDtypeStruct((n, hs), BF16),
                   jax.ShapeDtypeStruct((batch, N_HEADS, SEQ // TK, V_ROWS, TK), BF16),
                   jax.ShapeDtypeStruct((n, CONV_W), F32)],
        compiler_params=_params(1),
        name="proj",
    )(x2d, *consts, *tabs, *tabs_t)


def _attn_kernel(qt_ref, qt_next_ref, k_ref, vt_ref, o_ref, sta_ref, stb_ref, m_ref, acc_ref):
    qi = pl.program_id(1)
    m_ref[...] = jnp.full(m_ref.shape, NEG, F32)
    acc_ref[...] = jnp.zeros(acc_ref.shape, F32)
    key_pos = lax.broadcasted_iota(jnp.int32, (TK, TQ), 0)
    qry_chunk = lax.broadcasted_iota(jnp.int32, (TK, TQ), 1) // CHUNK
    diag_masks = [((key_pos + d * TK) // CHUNK <= qry_chunk)[:, d * TK:] for d in range(DIAG_BLOCKS)]

    def scores(kj, hd, dst_ref, q_ref=qt_ref, c0=0):
        k0 = pl.multiple_of(kj * TK, TK)
        sl = slice(hd * HEAD_SLAB, (hd + 1) * HEAD_SLAB)
        dst_ref[hd, :, c0:] = _dot(k_ref[pl.ds(k0, TK), sl], q_ref[sl, c0:])

    def softmax_pv(kj, hd, src_ref, mask, c0):
        st = src_ref[hd, :, c0:]
        if mask is not None:
            st = jnp.where(mask, st, NEG)
        m_old = m_ref[hd, :, c0:]
        m_new = jnp.maximum(m_old, jnp.max(st, axis=0, keepdims=True))
        p = jnp.exp2(st - m_new).astype(BF16)
        acc_ref[hd, :, c0:] = jnp.exp2(m_old - m_new) * acc_ref[hd, :, c0:] + _dot(vt_ref[hd, kj], p)
        m_ref[hd, :, c0:] = m_new

    def stage(kj, cur_ref, nxt_ref, diag=None, next_diag=None, last=False):
        c0 = 0 if diag is None else diag * TK
        for hd in range(N_HEADS):
            if last:
                scores(0, hd, nxt_ref, qt_next_ref)
            else:
                scores(kj + 1, hd, nxt_ref, c0=0 if next_diag is None else next_diag * TK)
            softmax_pv(kj, hd, cur_ref, None if diag is None else diag_masks[diag], c0)

    @pl.when(qi == 0)
    def _():
        for hd in range(N_HEADS):
            scores(0, hd, sta_ref)

    bufs = (sta_ref, stb_ref)

    def body(t, carry):
        for s in range(ATTN_UNROLL):
            stage(ATTN_UNROLL * t + s, bufs[s % 2], bufs[(s + 1) % 2])
        return carry

    n_free = qi * DIAG_BLOCKS
    trips = n_free // ATTN_UNROLL
    lax.fori_loop(0, trips, body, 0)
    rest = n_free - trips * ATTN_UNROLL
    for r in range(0, ATTN_UNROLL, DIAG_BLOCKS):
        @pl.when(rest == r)
        def _(r=r):
            n_stages = r + DIAG_BLOCKS
            diag_of = lambda s: s - r if r <= s < n_stages else None
            for s in range(n_stages):
                stage(n_free - r + s, bufs[s % 2], bufs[(s + 1) % 2],
                      diag=diag_of(s), next_diag=diag_of(s + 1), last=s == n_stages - 1)

    for pair in range(N_HEADS // 2):
        halves = []
        for hd in (2 * pair, 2 * pair + 1):
            acc = acc_ref[hd]
            halves.append(acc[:V_DIM] / acc[V_DIM:V_DIM + 1])
        pair_t = jnp.concatenate(halves, axis=0)
        o_ref[:, pair * 2 * V_DIM:(pair + 1) * 2 * V_DIM] = pair_t.T.astype(BF16)


def _attn(qt, k, vt, batch):
    hs = N_HEADS * HEAD_SLAB
    k3 = k.reshape(batch, SEQ, hs)
    assert DIAG_BLOCKS % 2 == 0 and ATTN_UNROLL % DIAG_BLOCKS == 0
    last = SEQ // TQ - 1
    out = pl.pallas_call(
        _attn_kernel,
        grid=(batch, SEQ // TQ),
        in_specs=[pl.BlockSpec((None, hs, TQ), lambda b, qi: (b, 0, qi)),
                  pl.BlockSpec((None, hs, TQ), lambda b, qi: (b, 0, jnp.minimum(qi + 1, last))),
                  pl.BlockSpec((None, SEQ, hs), lambda b, qi: (b, 0, 0)),
                  pl.BlockSpec((None, N_HEADS, SEQ // TK, V_ROWS, TK), lambda b, qi: (b, 0, 0, 0, 0))],
        out_specs=pl.BlockSpec((None, TQ, ATTN_W), lambda b, qi: (b, qi, 0)),
        out_shape=jax.ShapeDtypeStruct((batch, SEQ, ATTN_W), BF16),
        scratch_shapes=[pltpu.VMEM((N_HEADS, TK, TQ), F32), pltpu.VMEM((N_HEADS, TK, TQ), F32),
                        pltpu.VMEM((N_HEADS, 1, TQ), F32),
                        pltpu.VMEM((N_HEADS, V_ROWS, TQ), F32)],
        compiler_params=_params(2),
        name="attn",
    )(qt, qt, k3, vt)
    return out.reshape(batch * SEQ, ATTN_W)


def _conv_module(halo, u, cw_ref, cb, lg, lb, ext_ref, dst_ref):
    rows = u.shape[0]
    ext_ref[0, 0:HALO, :] = halo
    ext_ref[0, HALO:HALO + rows, :] = u
    span = HALO + rows - SUBLANES
    for b in range(1, SUBLANES):
        ext_ref[b, 0:span, :] = ext_ref[0, b:b + span, :]

    base = HALO - (CONV_K - 1)
    groups = CONV_ROWS // SUBLANES
    for r in range(0, rows, CONV_ROWS):
        accs = [jnp.zeros((SUBLANES, CONV_W), F32) for _ in range(groups)]
        for j in range(CONV_K):
            b = (base + j) % SUBLANES
            a = r + base + j - b
            w = cw_ref[j]
            for g in range(groups):
                accs[g] = accs[g] + w * ext_ref[b, a + g * SUBLANES:a + (g + 1) * SUBLANES, :]
        y = jnp.concatenate(accs, axis=0) + cb
        mu = jnp.mean(y, axis=-1, keepdims=True)
        yc = y - mu
        z = yc * lax.rsqrt(jnp.mean(yc * yc, axis=-1, keepdims=True) + EPS) * lg + lb
        dst_ref[r:r + CONV_ROWS, :] = (z * jax.nn.sigmoid(z)).astype(BF16)


def _mix_ffn_kernel(halo_ref, u_ref, attn_ref, x_ref, cw_ref, cb_ref, lg_ref, lb_ref, woa_ref, woc_ref,
                    g_ref, wg_ref, wu_ref, wd_ref, pg_ref, o_ref, ext_ref, *conv_refs):
    first = (pl.program_id(0) % (SEQ // TM_MIX)) == 0
    cb, lg, lb = cb_ref[...], lg_ref[...], lb_ref[...]
    part = TM_MIX // MIX_PARTS
    for p, conv_ref in enumerate(conv_refs):
        rows = slice(p * part, (p + 1) * part)
        if p == 0:
            halo = jnp.where(first, 0.0, halo_ref[...])
        else:
            halo = u_ref[p * part - HALO:p * part, :]
        _conv_module(halo, u_ref[rows, :], cw_ref, cb, lg, lb, ext_ref, conv_ref)
        x = x_ref[rows, :] + _dot(attn_ref[rows, :], woa_ref[...]) + _dot(conv_ref[...], woc_ref[...])
        o_ref[rows, :] = _rms(_half_step_ffn(x, g_ref[...], wg_ref, wu_ref, wd_ref), pg_ref[...])


def _mix_ffn(u, attn, x2d, cw, cb, lg, lb, woa, woc, g, wg, wu, wd, pg):
    n, d = x2d.shape
    tm = TM_MIX
    part = tm // MIX_PARTS
    row = lambda w: pl.BlockSpec((tm, w), lambda i: (i, 0))
    halo = pl.BlockSpec((HALO, CONV_W), lambda i: (jnp.maximum(i * (tm // HALO) - 1, 0), 0))
    consts = [cw, cb, lg, lb, woa, woc, g.reshape(1, d), wg, wu, wd, pg.reshape(1, d)]
    return pl.pallas_call(
        _mix_ffn_kernel,
        grid=(n // tm,),
        in_specs=[halo, row(CONV_W), row(ATTN_W), row(d)] + [_const_spec(a.shape) for a in consts],
        out_specs=row(d),
        out_shape=jax.ShapeDtypeStruct((n, d), F32),
        scratch_shapes=[pltpu.VMEM((SUBLANES, HALO + part, CONV_W), F32)]
                       + [pltpu.VMEM((part, CONV_W), BF16)] * MIX_PARTS,
        compiler_params=_params(1),
        name="mix_ffn",
    )(u, u, attn, x2d, *consts)


def _rope_slab_tables(seq_len):
    pos = jnp.arange(seq_len, dtype=F32)
    inv_freq = 1.0 / (ROPE_THETA ** (jnp.arange(0, QK_ROPE, 2, dtype=F32) / QK_ROPE))
    ang = pos[:, None] * inv_freq[None, :]
    cos, sin = jnp.cos(ang), jnp.sin(ang)
    half = QK_ROPE // 2
    ones = jnp.ones((seq_len, QK_NOPE), F32)
    z = lambda w: jnp.zeros((seq_len, w), F32)
    tail = HEAD_SLAB - QK_DIM
    c_tab = jnp.concatenate([ones, cos, cos, z(tail)], axis=1)
    sa_tab = jnp.concatenate([z(QK_NOPE), -sin, z(half), z(tail)], axis=1)
    sb_tab = jnp.concatenate([z(QK_NOPE), z(half), sin, z(tail)], axis=1)
    return (c_tab, sa_tab, sb_tab), (cos.T, sin.T)


def _head_slabs(w, width, offset=0):
    kdim = w.shape[0]
    w3 = w.reshape(kdim, N_HEADS, width)
    w3 = jnp.pad(w3, ((0, 0), (0, 0), (offset, HEAD_SLAB - width - offset)))
    return w3.reshape(kdim, N_HEADS * HEAD_SLAB)


def _slab_vec(g, offset=0):
    return jnp.pad(g, (offset, HEAD_SLAB - g.shape[0] - offset)).reshape(1, HEAD_SLAB)


def kernel(x, ffn1_norm, ffn1_w_gate, ffn1_w_up, ffn1_w_down, mix_norm, w_in, q_latent_norm, w_uq,
           kv_latent_norm, w_ukv, q_norm, k_norm, conv_w, conv_b, conv_ln_g, conv_ln_b, w_out,
           ffn2_norm, ffn2_w_gate, ffn2_w_up, ffn2_w_down, post_norm):
    batch, seq, d = x.shape
    assert (seq, d) == (SEQ, D_MODEL)
    depth = ffn1_norm.shape[0]
    tabs, tabs_t = _rope_slab_tables(seq)
    o_kv, o_pe, o_cv = Q_LORA, Q_LORA + KV_LORA, Q_LORA + KV_LORA + QK_ROPE
    bf = lambda a: a.astype(BF16)
    row = lambda a: a.reshape(1, -1)
    lanes = lambda a: jnp.broadcast_to(a[:, None], (a.shape[0], TK))

    x2d = x.reshape(batch * seq, d)
    for l in range(depth):
        x2d = _ffn(x2d, ffn1_norm[l], *_ffn_weights_bf16(l, ffn1_w_gate, ffn1_w_up, ffn1_w_down))

        wi = w_in[l]
        wpe = jnp.pad(wi[:, o_pe:o_cv], ((0, 0), (QK_NOPE, HEAD_SLAB - QK_DIM)))
        ukv = w_ukv[l].reshape(KV_LORA, N_HEADS, QK_NOPE + V_DIM)
        wk = _head_slabs(ukv[:, :, :QK_NOPE].reshape(KV_LORA, N_HEADS * QK_NOPE), QK_NOPE)
        wvt = ukv[:, :, QK_NOPE:].reshape(KV_LORA, ATTN_W).T
        qg = lanes(q_norm[l] * (QK_DIM ** -0.5 * LOG2E))
        qt, k, vt, u = _proj(
            x2d, row(mix_norm[l]), bf(wi[:, :o_kv].T), bf(wi[:, o_kv:o_pe]), bf(wpe), bf(wi[:, o_cv:]),
            lanes(q_latent_norm[l]), bf(_head_slabs(w_uq[l], QK_DIM).T), row(kv_latent_norm[l]), bf(wk), bf(wvt),
            qg, _slab_vec(k_norm[l]), tabs, tabs_t)

        attn = _attn(qt, k, vt, batch)

        wo = w_out[l]
        cw = jnp.broadcast_to(conv_w[l][:, None, :], (CONV_K, SUBLANES, CONV_W))
        x2d = _mix_ffn(u, attn, x2d, cw, row(conv_b[l]), row(conv_ln_g[l]), row(conv_ln_b[l]),
                       bf(wo[:ATTN_W]), bf(wo[ATTN_W:]),
                       ffn2_norm[l], *_ffn_weights_bf16(l, ffn2_w_gate, ffn2_w_up, ffn2_w_down), post_norm[l])
    return x2d.reshape(batch, seq, d)
```

```python
import jax
import jax.numpy as jnp
from jax import lax
from jax.experimental import pallas as pl
from jax.experimental.pallas import tpu as pltpu

D_MODEL = 1024
SEQ = 4096
CHUNK = 64
N_HEADS = 8
QK_NOPE = 64
QK_ROPE = 32
QK_DIM = QK_NOPE + QK_ROPE
V_DIM = 64
Q_LORA = 384
KV_LORA = 256
ATTN_W = N_HEADS * V_DIM
CONV_W = D_MODEL - ATTN_W
CONV_K = 31
D_FF = 2816
ROPE_THETA = 10000.0
EPS = 1e-6
NEG = -1e30

LANES = 128
SUBLANES = 8
HEAD_SLAB = LANES
HALO = 32
VMEM_LIMIT = 56 * 1024 * 1024

TM_FFN = 1024
TM_PROJ = 1024
TM_MIX = 512
TQ = 512
TK = 256
PROJ_PARTS = TM_PROJ // TK
DIAG_BLOCKS = TQ // TK
ATTN_UNROLL = 4
CONV_ROWS = 32
CAST_STEPS = 4
V_ROWS = 80
LOG2E = 1.4426950408889634
F_CHUNKS = ((0, 1024), (1024, 2048), (2048, D_FF))

BF16 = jnp.bfloat16
F32 = jnp.float32


def _rms(x, g):
    y = x * lax.rsqrt(jnp.mean(x * x, axis=-1, keepdims=True) + EPS)
    return y * g


def _dot(a, b):
    return jnp.dot(a, b, preferred_element_type=F32)


def _const_spec(shape):
    return pl.BlockSpec(shape, lambda *_: (0,) * len(shape), pipeline_mode=pl.Buffered(1))


def _params(n_axes):
    return pltpu.CompilerParams(dimension_semantics=("arbitrary",) * n_axes, vmem_limit_bytes=VMEM_LIMIT)


def _half_step_ffn(x, g, wg_ref, wu_ref, wd_ref):
    h = _rms(x, g).astype(BF16)
    y = jnp.zeros_like(x)
    for lo, hi in F_CHUNKS:
        gate = _dot(h, wg_ref[:, lo:hi])
        up = _dot(h, wu_ref[:, lo:hi])
        a = (gate * jax.nn.sigmoid(gate) * up).astype(BF16)
        y = y + _dot(a, wd_ref[lo:hi, :])
    return x + 0.5 * y


def _ffn_kernel(x_ref, g_ref, wg_ref, wu_ref, wd_ref, o_ref):
    o_ref[...] = _half_step_ffn(x_ref[...], g_ref[...], wg_ref, wu_ref, wd_ref)


def _ffn(x2d, g, wg, wu, wd):
    n, d = x2d.shape
    row = pl.BlockSpec((TM_FFN, d), lambda i: (i, 0))
    return pl.pallas_call(
        _ffn_kernel,
        grid=(n // TM_FFN,),
        in_specs=[row, _const_spec((1, d)), _const_spec(wg.shape), _const_spec(wu.shape), _const_spec(wd.shape)],
        out_specs=row,
        out_shape=jax.ShapeDtypeStruct((n, d), F32),
        compiler_params=_params(1),
        name="ffn",
    )(x2d, g.reshape(1, d), wg, wu, wd)


def _cast_kernel(*refs):
    n = len(refs) // 2
    for src, dst in zip(refs[:n], refs[n:]):
        dst[...] = src[...].astype(BF16)


def _ffn_weights_bf16(layer, *stacked):
    specs, out_specs, out_shapes = [], [], []
    for w in stacked:
        _, rows, cols = w.shape
        blk = rows // CAST_STEPS
        specs.append(pl.BlockSpec((None, blk, cols), lambda i, layer=layer: (layer, i, 0)))
        out_specs.append(pl.BlockSpec((blk, cols), lambda i: (i, 0)))
        out_shapes.append(jax.ShapeDtypeStruct((rows, cols), BF16))
    return pl.pallas_call(
        _cast_kernel,
        grid=(CAST_STEPS,),
        in_specs=specs,
        out_specs=out_specs,
        out_shape=out_shapes,
        compiler_params=_params(1),
        name="cast_w",
    )(*stacked)


def _rope(x, c, sa, sb):
    return x * c + pltpu.roll(x, HEAD_SLAB - QK_ROPE // 2, 1) * sa + pltpu.roll(x, QK_ROPE // 2, 1) * sb


def _proj_kernel(x_ref, g_ref, wqt_ref, wkv_ref, wpe_ref, wcv_ref, qlg_ref, wuqt_ref, kvlg_ref, wk_ref, wvt_ref,
                 qg_ref, kg_ref, c_ref, sa_ref, sb_ref, cos_ref, sin_ref, qt_ref, k_ref, vt_ref, u_ref):
    nt = (((1,), (1,)), ((), ()))
    half = QK_ROPE // 2
    r1, r2 = QK_NOPE, QK_NOPE + half
    qg, kg = qg_ref[...], kg_ref[...]
    slab_pad = jnp.zeros((HEAD_SLAB - QK_DIM, TK), BF16)
    ones_row = jnp.where(lax.broadcasted_iota(jnp.int32, (V_ROWS - V_DIM, TK), 0) == 0, 1.0, 0.0).astype(BF16)

    for p in range(PROJ_PARTS):
        rows = slice(p * TK, (p + 1) * TK)
        h = _rms(x_ref[rows, :], g_ref[...]).astype(BF16)
        c, sa, sb = c_ref[rows, :], sa_ref[rows, :], sb_ref[rows, :]
        cos_t, sin_t = cos_ref[:, rows], sin_ref[:, rows]

        cqt_raw = lax.dot_general(wqt_ref[...], h, nt, preferred_element_type=F32)
        ckv_raw = _dot(h, wkv_ref[...])
        kpe = _dot(h, wpe_ref[...])
        cv_a = _dot(h, wcv_ref[:, :CONV_W])
        cqt = cqt_raw * lax.rsqrt(jnp.mean(cqt_raw * cqt_raw, axis=0, keepdims=True) + EPS) * qlg_ref[...]
        ckv = _rms(ckv_raw, kvlg_ref[...]).astype(BF16)
        qft = _dot(wuqt_ref[...], cqt.astype(BF16))
        kf = _dot(ckv, wk_ref[...])
        vt = lax.dot_general(wvt_ref[...], ckv, nt, preferred_element_type=F32)
        cv_g = _dot(h, wcv_ref[:, CONV_W:])

        for hd in range(N_HEADS):
            base = hd * HEAD_SLAB
            qh = qft[base:base + QK_DIM, :]
            qn = qh * lax.rsqrt(jnp.sum(qh * qh, axis=0, keepdims=True) / QK_DIM + EPS) * qg
            x1, x2 = qn[r1:r2], qn[r2:QK_DIM]
            qt_ref[base:base + r1, rows] = qn[:r1].astype(BF16)
            qt_ref[base + r1:base + r2, rows] = (x1 * cos_t - x2 * sin_t).astype(BF16)
            qt_ref[base + r2:base + QK_DIM, rows] = (x1 * sin_t + x2 * cos_t).astype(BF16)
            qt_ref[base + QK_DIM:base + HEAD_SLAB, rows] = slab_pad

        for hd in range(N_HEADS):
            vt_ref[hd, p, 0:V_DIM, :] = vt[hd * V_DIM:(hd + 1) * V_DIM, :].astype(BF16)
            vt_ref[hd, p, V_DIM:V_ROWS, :] = ones_row
        pe_ss = jnp.sum(kpe * kpe, axis=-1, keepdims=True)
        pe_rot = _rope(kpe * kg, c, sa, sb)
        for hd in range(N_HEADS):
            sl = slice(hd * HEAD_SLAB, (hd + 1) * HEAD_SLAB)
            kh = kf[:, sl]
            rs = lax.rsqrt((jnp.sum(kh * kh, axis=-1, keepdims=True) + pe_ss) / QK_DIM + EPS)
            k_ref[rows, sl] = ((kh * kg + pe_rot) * rs).astype(BF16)

        u_ref[rows, :] = cv_a * jax.nn.sigmoid(cv_g)


def _proj(x2d, g, wqt, wkv, wpe, wcv, qlg, wuqt, kvlg, wk, wvt, qg, kg, tabs, tabs_t):
    n, d = x2d.shape
    tm = TM_PROJ
    s_blocks = SEQ // tm
    batch = n // SEQ
    hs = N_HEADS * HEAD_SLAB
    row = lambda w: pl.BlockSpec((tm, w), lambda i: (i, 0))
    tab = pl.BlockSpec((tm, HEAD_SLAB), lambda i: (i % s_blocks, 0))
    tab_t = pl.BlockSpec((QK_ROPE // 2, tm), lambda i: (0, i % s_blocks))
    qt_spec = pl.BlockSpec((None, hs, tm), lambda i: (i // s_blocks, 0, i % s_blocks))
    vt_spec = pl.BlockSpec((None, N_HEADS, PROJ_PARTS, V_ROWS, TK),
                           lambda i: (i // s_blocks, 0, i % s_blocks, 0, 0))
    consts = [g, wqt, wkv, wpe, wcv, qlg, wuqt, kvlg, wk, wvt, qg, kg]
    in_specs = [row(d)] + [_const_spec(a.shape) for a in consts] + [tab] * 3 + [tab_t] * 2
    return pl.pallas_call(
        _proj_kernel,
        grid=(n // tm,),
        in_specs=in_specs,
        out_specs=[qt_spec, row(hs), vt_spec, row(CONV_W)],
        out_shape=[jax.ShapeDtypeStruct((batch, hs, SEQ), BF16), jax.ShapeDtypeStruct((n, hs), BF16),
                   jax.ShapeDtypeStruct((batch, N_HEADS, SEQ // TK, V_ROWS, TK), BF16),
                   jax.ShapeDtypeStruct((n, CONV_W), F32)],
        compiler_params=_params(1),
        name="proj",
    )(x2d, *consts, *tabs, *tabs_t)


def _attn_kernel(qt_ref, qt_next_ref, k_ref, vt_ref, o_ref, sta_ref, stb_ref, m_ref, acc_ref):
    qi = pl.program_id(1)
    m_ref[...] = jnp.full(m_ref.shape, NEG, F32)
    acc_ref[...] = jnp.zeros(acc_ref.shape, F32)
    key_pos = lax.broadcasted_iota(jnp.int32, (TK, TQ), 0)
    qry_chunk = lax.broadcasted_iota(jnp.int32, (TK, TQ), 1) // CHUNK
    diag_masks = [((key_pos + d * TK) // CHUNK <= qry_chunk)[:, d * TK:] for d in range(DIAG_BLOCKS)]

    def scores(kj, hd, dst_ref, q_ref=qt_ref, c0=0):
        k0 = pl.multiple_of(kj * TK, TK)
        sl = slice(hd * HEAD_SLAB, (hd + 1) * HEAD_SLAB)
        dst_ref[hd, :, c0:] = _dot(k_ref[pl.ds(k0, TK), sl], q_ref[sl, c0:])

    def softmax_pv(kj, hd, src_ref, mask, c0):
        st = src_ref[hd, :, c0:]
        if mask is not None:
            st = jnp.where(mask, st, NEG)
        m_old = m_ref[hd, :, c0:]
        m_new = jnp.maximum(m_old, jnp.max(st, axis=0, keepdims=True))
        p = jnp.exp2(st - m_new).astype(BF16)
        acc_ref[hd, :, c0:] = jnp.exp2(m_old - m_new) * acc_ref[hd, :, c0:] + _dot(vt_ref[hd, kj], p)
        m_ref[hd, :, c0:] = m_new

    def stage(kj, cur_ref, nxt_ref, diag=None, next_diag=None, last=False):
        c0 = 0 if diag is None else diag * TK
        for hd in range(N_HEADS):
            if last:
                scores(0, hd, nxt_ref, qt_next_ref)
            else:
                scores(kj + 1, hd, nxt_ref, c0=0 if next_diag is None else next_diag * TK)
            softmax_pv(kj, hd, cur_ref, None if diag is None else diag_masks[diag], c0)

    @pl.when(qi == 0)
    def _():
        for hd in range(N_HEADS):
            scores(0, hd, sta_ref)

    bufs = (sta_ref, stb_ref)

    def body(t, carry):
        for s in range(ATTN_UNROLL):
            stage(ATTN_UNROLL * t + s, bufs[s % 2], bufs[(s + 1) % 2])
        return carry

    n_free = qi * DIAG_BLOCKS
    trips = n_free // ATTN_UNROLL
    lax.fori_loop(0, trips, body, 0)
    rest = n_free - trips * ATTN_UNROLL
    for r in range(0, ATTN_UNROLL, DIAG_BLOCKS):
        @pl.when(rest == r)
        def _(r=r):
            n_stages = r + DIAG_BLOCKS
            diag_of = lambda s: s - r if r <= s < n_stages else None
            for s in range(n_stages):
                stage(n_free - r + s, bufs[s % 2], bufs[(s + 1) % 2],
                      diag=diag_of(s), next_diag=diag_of(s + 1), last=s == n_stages - 1)

    for hd in range(N_HEADS):
        acc = acc_ref[hd]
        o_ref[hd * V_DIM:(hd + 1) * V_DIM, :] = (acc[:V_DIM] / acc[V_DIM:V_DIM + 1]).astype(BF16)


def _attn(qt, k, vt, batch):
    hs = N_HEADS * HEAD_SLAB
    k3 = k.reshape(batch, SEQ, hs)
    assert DIAG_BLOCKS % 2 == 0 and ATTN_UNROLL % DIAG_BLOCKS == 0
    last = SEQ // TQ - 1
    out = pl.pallas_call(
        _attn_kernel,
        grid=(batch, SEQ // TQ),
        in_specs=[pl.BlockSpec((None, hs, TQ), lambda b, qi: (b, 0, qi)),
                  pl.BlockSpec((None, hs, TQ), lambda b, qi: (b, 0, jnp.minimum(qi + 1, last))),
                  pl.BlockSpec((None, SEQ, hs), lambda b, qi: (b, 0, 0)),
                  pl.BlockSpec((None, N_HEADS, SEQ // TK, V_ROWS, TK), lambda b, qi: (b, 0, 0, 0, 0))],
        out_specs=pl.BlockSpec((None, ATTN_W, TQ), lambda b, qi: (b, 0, qi)),
        out_shape=jax.ShapeDtypeStruct((batch, ATTN_W, SEQ), BF16),
        scratch_shapes=[pltpu.VMEM((N_HEADS, TK, TQ), F32), pltpu.VMEM((N_HEADS, TK, TQ), F32),
                        pltpu.VMEM((N_HEADS, 1, TQ), F32),
                        pltpu.VMEM((N_HEADS, V_ROWS, TQ), F32)],
        compiler_params=_params(2),
        name="attn",
    )(qt, qt, k3, vt)
    return out


def _conv_module(halo, u, cw_ref, cb, lg, lb, ext_ref, dst_ref):
    rows = u.shape[0]
    ext_ref[0, 0:HALO, :] = halo
    ext_ref[0, HALO:HALO + rows, :] = u
    span = HALO + rows - SUBLANES
    for b in range(1, SUBLANES):
        ext_ref[b, 0:span, :] = ext_ref[0, b:b + span, :]

    base = HALO - (CONV_K - 1)
    groups = CONV_ROWS // SUBLANES
    for r in range(0, rows, CONV_ROWS):
        accs = [jnp.zeros((SUBLANES, CONV_W), F32) for _ in range(groups)]
        for j in range(CONV_K):
            b = (base + j) % SUBLANES
            a = r + base + j - b
            w = cw_ref[j]
            for g in range(groups):
                accs[g] = accs[g] + w * ext_ref[b, a + g * SUBLANES:a + (g + 1) * SUBLANES, :]
        y = jnp.concatenate(accs, axis=0) + cb
        mu = jnp.mean(y, axis=-1, keepdims=True)
        yc = y - mu
        z = yc * lax.rsqrt(jnp.mean(yc * yc, axis=-1, keepdims=True) + EPS) * lg + lb
        dst_ref[r:r + CONV_ROWS, :] = (z * jax.nn.sigmoid(z)).astype(BF16)


def _mix_ffn_kernel(halo_ref, u_ref, attn_t_ref, x_ref, cw_ref, cb_ref, lg_ref, lb_ref, woa_ref, woc_ref,
                    g_ref, wg_ref, wu_ref, wd_ref, pg_ref, o_ref, ext_ref, conv_ref):
    first = (pl.program_id(0) % (SEQ // TM_MIX)) == 0
    cb, lg, lb = cb_ref[...], lg_ref[...], lb_ref[...]
    _conv_module(jnp.where(first, 0.0, halo_ref[...]), u_ref[...], cw_ref, cb, lg, lb, ext_ref, conv_ref)
    attn_proj = lax.dot_general(attn_t_ref[...], woa_ref[...], (((0,), (0,)), ((), ())), preferred_element_type=F32)
    x = x_ref[...] + attn_proj + _dot(conv_ref[...], woc_ref[...])
    o_ref[...] = _rms(_half_step_ffn(x, g_ref[...], wg_ref, wu_ref, wd_ref), pg_ref[...])


def _mix_ffn(u, attn_t, x2d, cw, cb, lg, lb, woa, woc, g, wg, wu, wd, pg):
    n, d = x2d.shape
    tm = TM_MIX
    s_blocks = SEQ // tm
    row = lambda w: pl.BlockSpec((tm, w), lambda i: (i, 0))
    halo = pl.BlockSpec((HALO, CONV_W), lambda i: (jnp.maximum(i * (tm // HALO) - 1, 0), 0))
    attn_spec = pl.BlockSpec((None, ATTN_W, tm), lambda i: (i // s_blocks, 0, i % s_blocks))
    consts = [cw, cb, lg, lb, woa, woc, g.reshape(1, d), wg, wu, wd, pg.reshape(1, d)]
    return pl.pallas_call(
        _mix_ffn_kernel,
        grid=(n // tm,),
        in_specs=[halo, row(CONV_W), attn_spec, row(d)] + [_const_spec(a.shape) for a in consts],
        out_specs=row(d),
        out_shape=jax.ShapeDtypeStruct((n, d), F32),
        scratch_shapes=[pltpu.VMEM((SUBLANES, HALO + tm, CONV_W), F32), pltpu.VMEM((tm, CONV_W), BF16)],
        compiler_params=_params(1),
        name="mix_ffn",
    )(u, u, attn_t, x2d, *consts)


def _rope_slab_tables(seq_len):
    pos = jnp.arange(seq_len, dtype=F32)
    inv_freq = 1.0 / (ROPE_THETA ** (jnp.arange(0, QK_ROPE, 2, dtype=F32) / QK_ROPE))
    ang = pos[:, None] * inv_freq[None, :]
    cos, sin = jnp.cos(ang), jnp.sin(ang)
    half = QK_ROPE // 2
    ones = jnp.ones((seq_len, QK_NOPE), F32)
    z = lambda w: jnp.zeros((seq_len, w), F32)
    tail = HEAD_SLAB - QK_DIM
    c_tab = jnp.concatenate([ones, cos, cos, z(tail)], axis=1)
    sa_tab = jnp.concatenate([z(QK_NOPE), -sin, z(half), z(tail)], axis=1)
    sb_tab = jnp.concatenate([z(QK_NOPE), z(half), sin, z(tail)], axis=1)
    return (c_tab, sa_tab, sb_tab), (cos.T, sin.T)


def _head_slabs(w, width, offset=0):
    kdim = w.shape[0]
    w3 = w.reshape(kdim, N_HEADS, width)
    w3 = jnp.pad(w3, ((0, 0), (0, 0), (offset, HEAD_SLAB - width - offset)))
    return w3.reshape(kdim, N_HEADS * HEAD_SLAB)


def _slab_vec(g, offset=0):
    return jnp.pad(g, (offset, HEAD_SLAB - g.shape[0] - offset)).reshape(1, HEAD_SLAB)


def kernel(x, ffn1_norm, ffn1_w_gate, ffn1_w_up, ffn1_w_down, mix_norm, w_in, q_latent_norm, w_uq,
           kv_latent_norm, w_ukv, q_norm, k_norm, conv_w, conv_b, conv_ln_g, conv_ln_b, w_out,
           ffn2_norm, ffn2_w_gate, ffn2_w_up, ffn2_w_down, post_norm):
    batch, seq, d = x.shape
    assert (seq, d) == (SEQ, D_MODEL)
    depth = ffn1_norm.shape[0]
    tabs, tabs_t = _rope_slab_tables(seq)
    o_kv, o_pe, o_cv = Q_LORA, Q_LORA + KV_LORA, Q_LORA + KV_LORA + QK_ROPE
    bf = lambda a: a.astype(BF16)
    row = lambda a: a.reshape(1, -1)
    lanes = lambda a: jnp.broadcast_to(a[:, None], (a.shape[0], TK))

    x2d = x.reshape(batch * seq, d)
    for l in range(depth):
        x2d = _ffn(x2d, ffn1_norm[l], *_ffn_weights_bf16(l, ffn1_w_gate, ffn1_w_up, ffn1_w_down))

        wi = w_in[l]
        wpe = jnp.pad(wi[:, o_pe:o_cv], ((0, 0), (QK_NOPE, HEAD_SLAB - QK_DIM)))
        ukv = w_ukv[l].reshape(KV_LORA, N_HEADS, QK_NOPE + V_DIM)
        wk = _head_slabs(ukv[:, :, :QK_NOPE].reshape(KV_LORA, N_HEADS * QK_NOPE), QK_NOPE)
        wvt = ukv[:, :, QK_NOPE:].reshape(KV_LORA, ATTN_W).T
        qg = lanes(q_norm[l] * (QK_DIM ** -0.5 * LOG2E))
        qt, k, vt, u = _proj(
            x2d, row(mix_norm[l]), bf(wi[:, :o_kv].T), bf(wi[:, o_kv:o_pe]), bf(wpe), bf(wi[:, o_cv:]),
            lanes(q_latent_norm[l]), bf(_head_slabs(w_uq[l], QK_DIM).T), row(kv_latent_norm[l]), bf(wk), bf(wvt),
            qg, _slab_vec(k_norm[l]), tabs, tabs_t)

        attn = _attn(qt, k, vt, batch)

        wo = w_out[l]
        cw = jnp.broadcast_to(conv_w[l][:, None, :], (CONV_K, SUBLANES, CONV_W))
        x2d = _mix_ffn(u, attn, x2d, cw, row(conv_b[l]), row(conv_ln_g[l]), row(conv_ln_b[l]),
                       bf(wo[:ATTN_W]), bf(wo[ATTN_W:]), ffn2_norm[l], *_ffn_weights_bf16(l, ffn2_w_gate, ffn2_w_up, ffn2_w_down), post_norm[l])
    return x2d.reshape(batch, seq, d)
```
